```python
import jax, jax.numpy as jnp
from jax import lax
import numpy as np

D_MODEL = 1024
BATCH = 2
SEQ = 8192
DEPTH = 2

N_A_LAYERS = DEPTH // 2
N_B_LAYERS = DEPTH - N_A_LAYERS
FOX_HEADS = 16
FOX_HEAD_DIM = D_MODEL // FOX_HEADS
FOX_WIDTH = FOX_HEADS * FOX_HEAD_DIM
FOX_IN_COLS = 3 * FOX_WIDTH + FOX_HEADS
MLA_HEADS = 8
QK_NOPE_DIM = 128
QK_ROPE_DIM = 64
V_HEAD_DIM = 128
Q_LORA_RANK = 384
KV_LORA_RANK = 256
ROPE_BASE = 10000.0
D_FF = 4 * D_MODEL
Q_BLOCK = 128
EPS = 1e-6

kernel_name = "yoco_fox_mla_hybrid"


def rms_norm(x, g):
    xf = x.astype(jnp.float32)
    y = xf * lax.rsqrt(jnp.mean(xf * xf, axis=-1, keepdims=True) + EPS)
    return (y * g.astype(jnp.float32)).astype(x.dtype)


def sq_relu_mlp(h, w_up, w_down):
    return jnp.square(jax.nn.relu(h @ w_up)) @ w_down


def rope_tables(seq_len, dim):
    inv = 1.0 / (ROPE_BASE ** (jnp.arange(0, dim, 2, dtype=jnp.float32) / dim))
    ang = jnp.arange(seq_len, dtype=jnp.float32)[:, None] * inv[None, :]
    return jnp.cos(ang), jnp.sin(ang)


def apply_rope(t, cos, sin):
    cos = cos.astype(t.dtype)
    sin = sin.astype(t.dtype)
    half = t.shape[-1] // 2
    t1, t2 = t[..., :half], t[..., half:]
    return jnp.concatenate([t1 * cos - t2 * sin, t2 * cos + t1 * sin], axis=-1)


def causal_block_attention(logits_fn, q_parts, v):
    B, S = v.shape[0], v.shape[1]
    nb = S // Q_BLOCK
    blocks = tuple(jnp.moveaxis(t.reshape((B, nb, Q_BLOCK) + t.shape[2:]), 1, 0) for t in q_parts)
    kpos = jnp.arange(S)

    def one_block(args):
        i, qb = args
        logits = logits_fn(qb)
        qpos = i * Q_BLOCK + jnp.arange(Q_BLOCK)
        logits = jnp.where(kpos[None, :] <= qpos[:, None], logits, -jnp.inf)
        p = jax.nn.softmax(logits, axis=-1).astype(v.dtype)
        return jnp.einsum('bhqk,bkhd->bqhd', p, v)

    out = lax.map(one_block, (jnp.arange(nb), blocks))
    return jnp.moveaxis(out, 0, 1).reshape((B, S) + out.shape[3:])


def fox_mixer(h, w_in, b_f, w_out):
    B, S, _ = h.shape
    proj = h @ w_in
    q = proj[..., :FOX_WIDTH].reshape(B, S, FOX_HEADS, FOX_HEAD_DIM)
    k = proj[..., FOX_WIDTH:2 * FOX_WIDTH].reshape(B, S, FOX_HEADS, FOX_HEAD_DIM)
    v = proj[..., 2 * FOX_WIDTH:3 * FOX_WIDTH].reshape(B, S, FOX_HEADS, FOX_HEAD_DIM)
    f_logit = proj[..., 3 * FOX_WIDTH:].astype(jnp.float32) + b_f.astype(jnp.float32)
    cum = jnp.cumsum(jax.nn.log_sigmoid(f_logit), axis=1)
    c_keys = jnp.transpose(cum, (0, 2, 1))
    scale = FOX_HEAD_DIM ** -0.5

    def logits_fn(qb):
        q_blk, c_blk = qb
        s = jnp.einsum('bqhd,bkhd->bhqk', q_blk, k, preferred_element_type=jnp.float32) * scale
        return s + jnp.transpose(c_blk, (0, 2, 1))[..., None] - c_keys[:, :, None, :]

    ctx = causal_block_attention(logits_fn, (q, cum), v)
    return ctx.reshape(B, S, FOX_WIDTH) @ w_out


def mla_shared_kv(stream, kv_norm_g, w_kv_a, kv_a_norm_g, w_kv_b, cos, sin):
    B, S, _ = stream.shape
    src = rms_norm(stream, kv_norm_g)
    kv_a = src @ w_kv_a
    c_kv = rms_norm(kv_a[..., :KV_LORA_RANK], kv_a_norm_g)
    k_rope = apply_rope(kv_a[..., KV_LORA_RANK:], cos, sin)
    kv_b = (c_kv @ w_kv_b).reshape(B, S, MLA_HEADS, QK_NOPE_DIM + V_HEAD_DIM)
    k_nope = kv_b[..., :QK_NOPE_DIM]
    v = kv_b[..., QK_NOPE_DIM:]
    return k_nope, k_rope, v


def mla_mixer(h, w_q_a, q_a_norm_g, w_q_b, w_out, k_nope, k_rope, v, cos, sin):
    B, S, _ = h.shape
    c_q = rms_norm(h @ w_q_a, q_a_norm_g)
    q = (c_q @ w_q_b).reshape(B, S, MLA_HEADS, QK_NOPE_DIM + QK_ROPE_DIM)
    q_nope = q[..., :QK_NOPE_DIM]
    q_rope = apply_rope(q[..., QK_NOPE_DIM:], cos[:, None, :], sin[:, None, :])
    scale = (QK_NOPE_DIM + QK_ROPE_DIM) ** -0.5

    def logits_fn(qb):
        qn, qr = qb
        s = jnp.einsum('bqhd,bkhd->bhqk', qn, k_nope, preferred_element_type=jnp.float32)
        s = s + jnp.einsum('bqhr,bkr->bhqk', qr, k_rope, preferred_element_type=jnp.float32)
        return s * scale

    ctx = causal_block_attention(logits_fn, (q_nope, q_rope), v)
    return ctx.reshape(B, S, MLA_HEADS * V_HEAD_DIM) @ w_out


def setup_inputs(seed: int = 0) -> dict:
    key = jax.random.key(seed)
    ks = jax.random.split(key, 24)

    def w(k, shape, fan_in):
        return jax.random.normal(k, shape, jnp.float32) * (fan_in ** -0.5)

    def gain(k, shape):
        return 1.0 + 0.02 * jax.random.normal(k, shape, jnp.float32)

    return {
        "x": jax.random.normal(ks[0], (BATCH, SEQ, D_MODEL), jnp.float32),
        "norm_mix_g": gain(ks[1], (DEPTH, D_MODEL)),
        "norm_ffn_g": gain(ks[2], (DEPTH, D_MODEL)),
        "fox_w_in": w(ks[3], (N_A_LAYERS, D_MODEL, FOX_IN_COLS), D_MODEL),
        "fox_b_f": 1.0 + 0.1 * jax.random.normal(ks[4], (N_A_LAYERS, FOX_HEADS), jnp.float32),
        "fox_w_out": w(ks[5], (N_A_LAYERS, FOX_WIDTH, D_MODEL), FOX_WIDTH),
        "kv_norm_g": gain(ks[6], (D_MODEL,)),
        "mla_w_kv_a": w(ks[7], (D_MODEL, KV_LORA_RANK + QK_ROPE_DIM), D_MODEL),
        "mla_kv_a_norm_g": gain(ks[8], (KV_LORA_RANK,)),
        "mla_w_kv_b": w(ks[9], (KV_LORA_RANK, MLA_HEADS * (QK_NOPE_DIM + V_HEAD_DIM)), KV_LORA_RANK),
        "mla_w_q_a": w(ks[10], (N_B_LAYERS, D_MODEL, Q_LORA_RANK), D_MODEL),
        "mla_q_a_norm_g": gain(ks[11], (N_B_LAYERS, Q_LORA_RANK)),
        "mla_w_q_b": w(ks[12], (N_B_LAYERS, Q_LORA_RANK, MLA_HEADS * (QK_NOPE_DIM + QK_ROPE_DIM)), Q_LORA_RANK),
        "mla_w_out": w(ks[13], (N_B_LAYERS, MLA_HEADS * V_HEAD_DIM, D_MODEL), MLA_HEADS * V_HEAD_DIM),
        "ffn_w_up": w(ks[14], (DEPTH, D_MODEL, D_FF), D_MODEL),
        "ffn_w_down": w(ks[15], (DEPTH, D_FF, D_MODEL), D_FF),
        "final_norm_g": gain(ks[16], (D_MODEL,)),
    }


def reference(x, norm_mix_g, norm_ffn_g, fox_w_in, fox_b_f, fox_w_out, kv_norm_g,
              mla_w_kv_a, mla_kv_a_norm_g, mla_w_kv_b, mla_w_q_a, mla_q_a_norm_g,
              mla_w_q_b, mla_w_out, ffn_w_up, ffn_w_down, final_norm_g):
    S = x.shape[1]
    cos, sin = rope_tables(S, QK_ROPE_DIM)
    k_nope = k_rope = v_shared = None
    for layer in range(DEPTH):
        h = rms_norm(x, norm_mix_g[layer])
        if layer < N_A_LAYERS:
            x = x + fox_mixer(h, fox_w_in[layer], fox_b_f[layer], fox_w_out[layer])
        else:
            b = layer - N_A_LAYERS
            x = x + mla_mixer(h, mla_w_q_a[b], mla_q_a_norm_g[b], mla_w_q_b[b], mla_w_out[b],
                              k_nope, k_rope, v_shared, cos, sin)
        x = x + sq_relu_mlp(rms_norm(x, norm_ffn_g[layer]), ffn_w_up[layer], ffn_w_down[layer])
        if layer == N_A_LAYERS - 1:
            k_nope, k_rope, v_shared = mla_shared_kv(x, kv_norm_g, mla_w_kv_a, mla_kv_a_norm_g,
                                                     mla_w_kv_b, cos, sin)
    return rms_norm(x, final_norm_g)
```

```python
import functools
import math

import jax
import jax.numpy as jnp
from jax import lax
from jax.experimental import pallas as pl
from jax.experimental.pallas import tpu as pltpu

D_MODEL = 1024
FOX_HEADS = 16
FOX_HEAD_DIM = 64
FOX_WIDTH = FOX_HEADS * FOX_HEAD_DIM
MLA_HEADS = 8
QK_NOPE_DIM = 128
QK_ROPE_DIM = 64
V_HEAD_DIM = 128
Q_LORA_RANK = 384
KV_LORA_RANK = 256
ROPE_BASE = 10000.0
D_FF = 4 * D_MODEL
EPS = 1e-6

V7X_LANES = 128
V7X_VMEM_BYTES = 64 * 1024 * 1024

LOG2E = 1.4426950408889634
FOX_QSCALE = (FOX_HEAD_DIM ** -0.5) * LOG2E
MLA_QSCALE = ((QK_NOPE_DIM + QK_ROPE_DIM) ** -0.5) * LOG2E
MLA_QK_PAD = 256

ROW_TILE = 512
SEQ_TILE = 512
FF_CHUNK = 512

BF16 = jnp.bfloat16
F32 = jnp.float32
_NT = (((1,), (1,)), ((), ()))


def _compiler_params(n_grid, pipelined_bytes, resident_bytes, temp_bytes):
    need = 2 * pipelined_bytes + resident_bytes + temp_bytes
    return pltpu.CompilerParams(
        dimension_semantics=("arbitrary",) * n_grid,
        vmem_limit_bytes=min(need, V7X_VMEM_BYTES),
    )


def _resident(shape, index_map):
    return pl.BlockSpec(shape, index_map, pipeline_mode=pl.Buffered(1))


def _rms_unit(xf):
    return xf * lax.rsqrt(jnp.mean(xf * xf, axis=-1, keepdims=True) + EPS)


def _dot(a, b):
    return jnp.dot(a, b, preferred_element_type=F32)


def _fox_inproj_kernel(x_ref, g_ref, w_ref, wf_ref, q_ref, k_ref, v_ref, f_ref):
    h = (_rms_unit(x_ref[...]) * g_ref[...]).astype(BF16)
    q_ref[...] = (_dot(h, w_ref[:, 0:FOX_WIDTH]) * FOX_QSCALE).astype(BF16)
    k_ref[...] = _dot(h, w_ref[:, FOX_WIDTH:2 * FOX_WIDTH]).astype(BF16)
    v_ref[...] = _dot(h, w_ref[:, 2 * FOX_WIDTH:3 * FOX_WIDTH]).astype(BF16)
    f_ref[...] = _dot(h, wf_ref[...])


def _fox_inproj(x2d, g, w_qkv, w_f):
    m = x2d.shape[0]
    tm = ROW_TILE
    row = lambda i: (i, 0)
    fixed = lambda i: (0, 0)
    out_bf = jax.ShapeDtypeStruct((m, FOX_WIDTH), BF16)
    pipelined = tm * D_MODEL * 4 + 3 * tm * FOX_WIDTH * 2 + tm * V7X_LANES * 4
    resident = D_MODEL * 4 + D_MODEL * 3 * FOX_WIDTH * 2 + D_MODEL * V7X_LANES * 2
    temps = tm * D_MODEL * (4 + 2) + 2 * tm * FOX_WIDTH * 4
    return pl.pallas_call(
        _fox_inproj_kernel,
        grid=(m // tm,),
        in_specs=[
            pl.BlockSpec((tm, D_MODEL), row),
            _resident((1, D_MODEL), fixed),
            _resident((D_MODEL, 3 * FOX_WIDTH), fixed),
            _resident((D_MODEL, V7X_LANES), fixed),
        ],
        out_specs=[
            pl.BlockSpec((tm, FOX_WIDTH), row),
            pl.BlockSpec((tm, FOX_WIDTH), row),
            pl.BlockSpec((tm, FOX_WIDTH), row),
            pl.BlockSpec((tm, V7X_LANES), row),
        ],
        out_shape=[out_bf, out_bf, out_bf, jax.ShapeDtypeStruct((m, V7X_LANES), F32)],
        compiler_params=_compiler_params(1, pipelined, resident, temps),
        name="fox_inproj",
    )(x2d, g, w_qkv, w_f)


def _gate_cumsum_kernel(f_ref, b_ref, c_ref, carry_ref, *, sb):
    @pl.when(pl.program_id(1) == 0)
    def _():
        carry_ref[...] = jnp.zeros_like(carry_ref)

    z = f_ref[0] + b_ref[...]
    ls = jnp.minimum(z, 0.0) - jnp.log1p(jnp.exp(-jnp.abs(z)))
    hi = ls.astype(BF16)
    rem = ls - hi.astype(F32)
    mid = rem.astype(BF16)
    lo = (rem - mid.astype(F32)).astype(BF16)
    row = lax.broadcasted_iota(jnp.int32, (sb, sb), 0)
    col = lax.broadcasted_iota(jnp.int32, (sb, sb), 1)
    tri = jnp.where(row >= col, 1.0, 0.0).astype(BF16)
    c = _dot(tri, hi) + _dot(tri, mid) + _dot(tri, lo) + carry_ref[...]
    carry_ref[...] = c[sb - 1:sb, :]
    c_ref[0] = (c * LOG2E).T[0:FOX_HEADS, :]


def _gate_cumsum(f_logit, b_f):
    b, s, _ = f_logit.shape
    sb = SEQ_TILE
    pipelined = sb * V7X_LANES * 4 + FOX_HEADS * sb * 4
    temps = 2 * sb * sb * 4 + 8 * sb * V7X_LANES * 4
    return pl.pallas_call(
        functools.partial(_gate_cumsum_kernel, sb=sb),
        grid=(b, s // sb),
        in_specs=[
            pl.BlockSpec((1, sb, V7X_LANES), lambda i, j: (i, j, 0)),
            _resident((1, V7X_LANES), lambda i, j: (0, 0)),
        ],
        out_specs=pl.BlockSpec((1, FOX_HEADS, sb), lambda i, j: (i, 0, j)),
        out_shape=jax.ShapeDtypeStruct((b, FOX_HEADS, s), F32),
        scratch_shapes=[pltpu.VMEM((1, V7X_LANES), F32)],
        compiler_params=_compiler_params(2, pipelined, V7X_LANES * 4, temps),
        name="gate_cumsum",
    )(f_logit, b_f)


def _softmax_block(s, v, m, l, acc):
    m_new = jnp.maximum(m, jnp.max(s, axis=-1, keepdims=True))
    alpha = jnp.exp2(m - m_new)
    p = jnp.exp2(s - m_new)
    l = alpha * l + jnp.sum(p, axis=-1, keepdims=True)
    acc = alpha * acc + _dot(p.astype(BF16), v)
    return m_new, l, acc


def _causal_keep(t):
    row = lax.broadcasted_iota(jnp.int32, (t, t), 0)
    col = lax.broadcasted_iota(jnp.int32, (t, t), 1)
    return col <= row


def _softmax_init(t, width):
    return (jnp.full((t, 1), -jnp.inf, F32), jnp.zeros((t, 1), F32), jnp.zeros((t, width), F32))


def _fox_attn_kernel(q_ref, k_ref, v_ref, c_ref, o_ref, *, t):
    qi = pl.program_id(2)
    q = q_ref[0]
    first = lax.broadcasted_iota(jnp.int32, (t, V7X_LANES), 1) < FOX_HEAD_DIM
    zero = jnp.zeros_like(q)
    q_head = (jnp.where(first, q, zero), jnp.where(first, zero, q))

    def block(j, carry, keep):
        ks = pl.multiple_of(j * t, t)
        k = k_ref[0, pl.ds(ks, t), :]
        v = v_ref[0, pl.ds(ks, t), :]
        out = []
        for h in range(2):
            s = lax.dot_general(q_head[h], k, _NT, preferred_element_type=F32)
            s = s - c_ref[0, 0, h:h + 1, pl.ds(ks, t)]
            if keep is not None:
                s = jnp.where(keep, s, -jnp.inf)
            out.append(_softmax_block(s, v, *carry[h]))
        return tuple(out)

    init = (_softmax_init(t, V7X_LANES),) * 2
    carry = lax.fori_loop(0, qi, lambda j, c: block(j, c, None), init)
    (_, l0, acc0), (_, l1, acc1) = block(qi, carry, _causal_keep(t))
    o_ref[0] = jnp.where(first, acc0 / l0, acc1 / l1).astype(o_ref.dtype)


def _fox_attn(q, k, v, c_pairs):
    b, s, _ = q.shape
    t = SEQ_TILE
    pairs = FOX_HEADS // 2
    pipelined = 2 * t * V7X_LANES * 2 + 2 * s * V7X_LANES * 2 + 2 * s * 4
    temps = 2 * (t * t * (4 + 4 + 2) + 3 * t * V7X_LANES * 4)
    return pl.pallas_call(
        functools.partial(_fox_attn_kernel, t=t),
        grid=(b, pairs, s // t),
        in_specs=[
            pl.BlockSpec((1, t, V7X_LANES), lambda i, p, j: (i, j, p)),
            pl.BlockSpec((1, s, V7X_LANES), lambda i, p, j: (i, 0, p)),
            pl.BlockSpec((1, s, V7X_LANES), lambda i, p, j: (i, 0, p)),
            pl.BlockSpec((1, 1, 2, s), lambda i, p, j: (i, p, 0, 0)),
        ],
        out_specs=pl.BlockSpec((1, t, V7X_LANES), lambda i, p, j: (i, j, p)),
        out_shape=jax.ShapeDtypeStruct((b, s, FOX_WIDTH), BF16),
        compiler_params=_compiler_params(3, pipelined, 0, temps),
        name="fox_attn",
    )(q, k, v, c_pairs)


def _mix_ffn_kernel(x_ref, ctx_ref, wo_ref, g_ref, wu_ref, wd_ref, gf_ref, o_ref, *, final_norm):
    x1 = x_ref[...] + _dot(ctx_ref[...], wo_ref[...])
    hn = (_rms_unit(x1) * g_ref[...]).astype(BF16)
    acc = x1
    for c in range(D_FF // FF_CHUNK):
        cols = slice(c * FF_CHUNK, (c + 1) * FF_CHUNK)
        up = jnp.maximum(_dot(hn, wu_ref[:, cols]), 0.0)
        acc = acc + _dot((up * up).astype(BF16), wd_ref[cols, :])
    if final_norm:
        acc = _rms_unit(acc) * gf_ref[...]
    o_ref[...] = acc


def _mix_ffn(x2d, ctx2d, w_out, g_ffn, w_up, w_down, g_final, final_norm):
    m = x2d.shape[0]
    tm = ROW_TILE
    row = lambda i: (i, 0)
    fixed = lambda i: (0, 0)
    pipelined = tm * D_MODEL * (4 + 2 + 4)
    resident = (D_MODEL * D_MODEL + 2 * D_MODEL * D_FF) * 2 + 2 * D_MODEL * 4
    temps = tm * D_MODEL * (4 + 4 + 2) + tm * FF_CHUNK * (4 + 4 + 2)
    return pl.pallas_call(
        functools.partial(_mix_ffn_kernel, final_norm=final_norm),
        grid=(m // tm,),
        in_specs=[
            pl.BlockSpec((tm, D_MODEL), row),
            pl.BlockSpec((tm, D_MODEL), row),
            _resident((D_MODEL, D_MODEL), fixed),
            _resident((1, D_MODEL), fixed),
            _resident((D_MODEL, D_FF), fixed),
            _resident((D_FF, D_MODEL), fixed),
            _resident((1, D_MODEL), fixed),
        ],
        out_specs=pl.BlockSpec((tm, D_MODEL), row),
        out_shape=jax.ShapeDtypeStruct((m, D_MODEL), F32),
        compiler_params=_compiler_params(1, pipelined, resident, temps),
        name="mix_ffn_final" if final_norm else "mix_ffn",
    )(x2d, ctx2d, w_out, g_ffn, w_up, w_down, g_final)


def _rope_lanes(t, cos, sin_signed):
    lane = lax.broadcasted_iota(jnp.int32, t.shape, 1)
    low = (lane % QK_ROPE_DIM) < (QK_ROPE_DIM // 2)
    partner = jnp.where(low,
                        pltpu.roll(t, V7X_LANES - QK_ROPE_DIM // 2, 1),
                        pltpu.roll(t, QK_ROPE_DIM // 2, 1))
    return t * cos + partner * sin_signed


def _mla_proj_kernel(x_ref, gkv_ref, gq_ref, wkva_ref, gkva_ref, wkvb_ref, wqa_ref, gqa_ref,
                     wqb_ref, cos_ref, sin_ref, qcat_ref, kcat_ref, v_ref):
    unit = _rms_unit(x_ref[...])
    cos = cos_ref[...]
    sin = sin_ref[...]
    tm = unit.shape[0]
    first = lax.broadcasted_iota(jnp.int32, (tm, V7X_LANES), 1) < QK_ROPE_DIM

    src = (unit * gkv_ref[...]).astype(BF16)
    kv_a = _dot(src, wkva_ref[...])
    c_kv = (_rms_unit(kv_a[:, 0:KV_LORA_RANK]) * gkva_ref[...]).astype(BF16)
    k_rope = _rope_lanes(kv_a[:, KV_LORA_RANK:KV_LORA_RANK + V7X_LANES], cos, sin)
    k_rope_even = k_rope.astype(BF16)
    k_rope_odd = pltpu.roll(k_rope, QK_ROPE_DIM, 1).astype(BF16)
    k_nope = _dot(c_kv, wkvb_ref[:, 0:MLA_HEADS * QK_NOPE_DIM]).astype(BF16)
    v_ref[...] = _dot(c_kv, wkvb_ref[:, MLA_HEADS * QK_NOPE_DIM:]).astype(BF16)

    h = (unit * gq_ref[...]).astype(BF16)
    c_q = (_rms_unit(_dot(h, wqa_ref[...])) * gqa_ref[...]).astype(BF16)
    q_nope = (_dot(c_q, wqb_ref[:, 0:MLA_HEADS * QK_NOPE_DIM]) * MLA_QSCALE).astype(BF16)
    q_rope_all = _dot(c_q, wqb_ref[:, MLA_HEADS * QK_NOPE_DIM:]) * MLA_QSCALE
    zero = jnp.zeros((tm, V7X_LANES), BF16)

    for pair in range(MLA_HEADS // 2):
        lanes = slice(pair * V7X_LANES, (pair + 1) * V7X_LANES)
        q_rope = _rope_lanes(q_rope_all[:, lanes], cos, sin).astype(BF16)
        for odd in range(2):
            head = 2 * pair + odd
            nope = slice(head * QK_NOPE_DIM, (head + 1) * QK_NOPE_DIM)
            lo = head * MLA_QK_PAD
            qcat_ref[:, lo:lo + QK_NOPE_DIM] = q_nope[:, nope]
            kcat_ref[:, lo:lo + QK_NOPE_DIM] = k_nope[:, nope]
            if odd:
                qcat_ref[:, lo + QK_NOPE_DIM:lo + MLA_QK_PAD] = jnp.where(first, zero, q_rope)
                kcat_ref[:, lo + QK_NOPE_DIM:lo + MLA_QK_PAD] = k_rope_odd
            else:
                qcat_ref[:, lo + QK_NOPE_DIM:lo + MLA_QK_PAD] = jnp.where(first, q_rope, zero)
                kcat_ref[:, lo + QK_NOPE_DIM:lo + MLA_QK_PAD] = k_rope_even


def _mla_proj(x2d, g_kv, g_q, w_kv_a, g_kv_a, w_kv_b, w_q_a, g_q_a, w_q_b, cos_t, sin_t, seq):
    m = x2d.shape[0]
    tm = ROW_TILE
    row = lambda i: (i, 0)
    fixed = lambda i: (0, 0)
    pos = lambda i: (i % (seq // tm), 0)
    kva_cols = w_kv_a.shape[1]
    cat = MLA_HEADS * MLA_QK_PAD
    pipelined = tm * D_MODEL * 4 + 2 * tm * V7X_LANES * 4 + tm * (2 * cat + D_MODEL) * 2
    resident = (D_MODEL * kva_cols + KV_LORA_RANK * 2 * D_MODEL + D_MODEL * Q_LORA_RANK
                + Q_LORA_RANK * 3 * D_MODEL // 2) * 2 + 4 * D_MODEL * 4
    temps = tm * D_MODEL * (4 + 2 + 2) + 4 * tm * D_MODEL * 4
    return pl.pallas_call(
        _mla_proj_kernel,
        grid=(m // tm,),
        in_specs=[
            pl.BlockSpec((tm, D_MODEL), row),
            _resident((1, D_MODEL), fixed),
            _resident((1, D_MODEL), fixed),
            _resident((D_MODEL, kva_cols), fixed),
            _resident((1, KV_LORA_RANK), fixed),
            _resident((KV_LORA_RANK, 2 * D_MODEL), fixed),
            _resident((D_MODEL, Q_LORA_RANK), fixed),
            _resident((1, Q_LORA_RANK), fixed),
            _resident((Q_LORA_RANK, 3 * D_MODEL // 2), fixed),
            pl.BlockSpec((tm, V7X_LANES), pos),
            pl.BlockSpec((tm, V7X_LANES), pos),
        ],
        out_specs=[
            pl.BlockSpec((tm, cat), row),
            pl.BlockSpec((tm, cat), row),
            pl.BlockSpec((tm, D_MODEL), row),
        ],
        out_shape=[
            jax.ShapeDtypeStruct((m, cat), BF16),
            jax.ShapeDtypeStruct((m, cat), BF16),
            jax.ShapeDtypeStruct((m, D_MODEL), BF16),
        ],
        compiler_params=_compiler_params(1, pipelined, resident, temps),
        name="mla_proj",
    )(x2d, g_kv, g_q, w_kv_a, g_kv_a, w_kv_b, w_q_a, g_q_a, w_q_b, cos_t, sin_t)


def _mla_attn_kernel(q_ref, k_ref, v_ref, o_ref, *, t):
    qi = pl.program_id(2)
    q = q_ref[0]

    def block(j, carry, keep):
        ks = pl.multiple_of(j * t, t)
        s = lax.dot_general(q, k_ref[0, pl.ds(ks, t), :], _NT, preferred_element_type=F32)
        if keep is not None:
            s = jnp.where(keep, s, -jnp.inf)
        return _softmax_block(s, v_ref[0, pl.ds(ks, t), :], *carry)

    carry = lax.fori_loop(0, qi, lambda j, c: block(j, c, None), _softmax_init(t, V_HEAD_DIM))
    _, l, acc = block(qi, carry, _causal_keep(t))
    o_ref[0] = (acc / l).astype(o_ref.dtype)


def _mla_attn(q_cat, k_cat, v):
    b, s, _ = v.shape
    t = SEQ_TILE
    pipelined = t * MLA_QK_PAD * 2 + s * MLA_QK_PAD * 2 + s * V_HEAD_DIM * 2 + t * V_HEAD_DIM * 2
    temps = t * t * (4 + 4 + 2) + 3 * t * V7X_LANES * 4
    return pl.pallas_call(
        functools.partial(_mla_attn_kernel, t=t),
        grid=(b, MLA_HEADS, s // t),
        in_specs=[
            pl.BlockSpec((1, t, MLA_QK_PAD), lambda i, h, j: (i, j, h)),
            pl.BlockSpec((1, s, MLA_QK_PAD), lambda i, h, j: (i, 0, h)),
            pl.BlockSpec((1, s, V_HEAD_DIM), lambda i, h, j: (i, 0, h)),
        ],
        out_specs=pl.BlockSpec((1, t, V_HEAD_DIM), lambda i, h, j: (i, j, h)),
        out_shape=jax.ShapeDtypeStruct((b, s, MLA_HEADS * V_HEAD_DIM), BF16),
        compiler_params=_compiler_params(3, pipelined, 0, temps),
        name="mla_attn",
    )(q_cat, k_cat, v)


def _rope_tables(seq):
    half = QK_ROPE_DIM // 2
    inv = 1.0 / (ROPE_BASE ** (jnp.arange(0, QK_ROPE_DIM, 2, dtype=F32) / QK_ROPE_DIM))
    ang = jnp.arange(seq, dtype=F32)[:, None] * inv[None, :]
    cos, sin = jnp.cos(ang), jnp.sin(ang)
    reps = V7X_LANES // half
    cos_t = jnp.tile(cos, (1, reps))
    sin_t = jnp.tile(jnp.concatenate([-sin, sin], axis=1), (1, reps // 2))
    return cos_t, sin_t


def _row(vec):
    return vec.reshape(1, -1).astype(F32)


def kernel(x, norm_mix_g, norm_ffn_g, fox_w_in, fox_b_f, fox_w_out, kv_norm_g, mla_w_kv_a,
           mla_kv_a_norm_g, mla_w_kv_b, mla_w_q_a, mla_q_a_norm_g, mla_w_q_b, mla_w_out,
           ffn_w_up, ffn_w_down, final_norm_g):
    b, s, d = x.shape
    assert d == D_MODEL and s % SEQ_TILE == 0 and (b * s) % ROW_TILE == 0
    assert fox_w_in.shape[0] == 1 and mla_w_q_a.shape[0] == 1
    x2d = x.reshape(b * s, d)

    w_in = fox_w_in[0]
    w_qkv = w_in[:, :3 * FOX_WIDTH].astype(BF16)
    w_f = jnp.pad(w_in[:, 3 * FOX_WIDTH:], ((0, 0), (0, V7X_LANES - FOX_HEADS))).astype(BF16)
    b_f = jnp.pad(fox_b_f[0].astype(F32), (0, V7X_LANES - FOX_HEADS)).reshape(1, V7X_LANES)
    q, k, v, f_logit = _fox_inproj(x2d, _row(norm_mix_g[0]), w_qkv, w_f)
    c2 = _gate_cumsum(f_logit.reshape(b, s, V7X_LANES), b_f)
    ctx = _fox_attn(q.reshape(b, s, FOX_WIDTH), k.reshape(b, s, FOX_WIDTH),
                    v.reshape(b, s, FOX_WIDTH), c2.reshape(b, FOX_HEADS // 2, 2, s))
    x2d = _mix_ffn(x2d, ctx.reshape(b * s, FOX_WIDTH), fox_w_out[0].astype(BF16),
                   _row(norm_ffn_g[0]), ffn_w_up[0].astype(BF16), ffn_w_down[0].astype(BF16),
                   _row(final_norm_g), final_norm=False)

    w_kv_a = jnp.pad(mla_w_kv_a, ((0, 0), (0, V7X_LANES - QK_ROPE_DIM))).astype(BF16)
    w_kv_b = mla_w_kv_b.reshape(KV_LORA_RANK, MLA_HEADS, QK_NOPE_DIM + V_HEAD_DIM)
    w_kv_b = jnp.concatenate(
        [w_kv_b[:, :, :QK_NOPE_DIM].reshape(KV_LORA_RANK, -1),
         w_kv_b[:, :, QK_NOPE_DIM:].reshape(KV_LORA_RANK, -1)], axis=1).astype(BF16)
    w_q_b = mla_w_q_b[0].reshape(Q_LORA_RANK, MLA_HEADS, QK_NOPE_DIM + QK_ROPE_DIM)
    w_q_b = jnp.concatenate(
        [w_q_b[:, :, :QK_NOPE_DIM].reshape(Q_LORA_RANK, -1),
         w_q_b[:, :, QK_NOPE_DIM:].reshape(Q_LORA_RANK, -1)], axis=1).astype(BF16)
    cos_t, sin_t = _rope_tables(s)
    q_cat, k_cat, v_mla = _mla_proj(
        x2d, _row(kv_norm_g), _row(norm_mix_g[1]), w_kv_a, _row(mla_kv_a_norm_g), w_kv_b,
        mla_w_q_a[0].astype(BF16), _row(mla_q_a_norm_g[0]), w_q_b, cos_t, sin_t, s)

    cat = MLA_HEADS * MLA_QK_PAD
    ctx = _mla_attn(q_cat.reshape(b, s, cat), k_cat.reshape(b, s, cat),
                    v_mla.reshape(b, s, MLA_HEADS * V_HEAD_DIM))
    out = _mix_ffn(x2d, ctx.reshape(b * s, MLA_HEADS * V_HEAD_DIM), mla_w_out[0].astype(BF16),
                   _row(norm_ffn_g[1]), ffn_w_up[1].astype(BF16), ffn_w_down[1].astype(BF16),
                   _row(final_norm_g), final_norm=True)
    return out.reshape(b, s, d)
```

```python
import functools
import math

import jax
import jax.numpy as jnp
import numpy as np
from jax import lax
from jax.experimental import pallas as pl
from jax.experimental.pallas import tpu as pltpu

D_MODEL = 1024
FOX_HEADS = 16
FOX_HEAD_DIM = 64
FOX_WIDTH = FOX_HEADS * FOX_HEAD_DIM
MLA_HEADS = 8
QK_NOPE_DIM = 128
QK_ROPE_DIM = 64
V_HEAD_DIM = 128
Q_LORA_RANK = 384
KV_LORA_RANK = 256
ROPE_BASE = 10000.0
D_FF = 4 * D_MODEL
EPS = 1e-6

V7X_LANES = 128
V7X_VMEM_BYTES = 64 * 1024 * 1024

LOG2E = 1.4426950408889634
FOX_QSCALE = (FOX_HEAD_DIM ** -0.5) * LOG2E
MLA_QSCALE = ((QK_NOPE_DIM + QK_ROPE_DIM) ** -0.5) * LOG2E
MLA_QK_PAD = 256

ROW_TILE = 512
SEQ_TILE = 512
FF_CHUNK = 512

BF16 = jnp.bfloat16
F32 = jnp.float32
_NT = (((1,), (1,)), ((), ()))


def _compiler_params(n_grid, pipelined_bytes, resident_bytes, temp_bytes):
    need = 2 * pipelined_bytes + resident_bytes + temp_bytes
    return pltpu.CompilerParams(
        dimension_semantics=("arbitrary",) * n_grid,
        vmem_limit_bytes=min(need, V7X_VMEM_BYTES),
    )


def _resident(shape, index_map):
    return pl.BlockSpec(shape, index_map, pipeline_mode=pl.Buffered(1))


def _rms_unit(xf):
    return xf * lax.rsqrt(jnp.mean(xf * xf, axis=-1, keepdims=True) + EPS)


def _dot(a, b):
    return jnp.dot(a, b, preferred_element_type=F32)


def _fox_inproj_kernel(x_ref, g_ref, w_ref, wvt_ref, wf_ref, q_ref, k_ref, vt_ref, f_ref):
    h = (_rms_unit(x_ref[...]) * g_ref[...]).astype(BF16)
    q_ref[...] = (_dot(h, w_ref[:, 0:FOX_WIDTH]) * FOX_QSCALE).astype(BF16)
    k_ref[...] = _dot(h, w_ref[:, FOX_WIDTH:2 * FOX_WIDTH]).astype(BF16)
    vt_ref[...] = lax.dot_general(wvt_ref[...], h, _NT, preferred_element_type=F32).astype(BF16)
    f_ref[...] = _dot(h, wf_ref[...])


def _fox_inproj(x2d, g, w_qk, w_vt, w_f):
    m = x2d.shape[0]
    tm = ROW_TILE
    row = lambda i: (i, 0)
    fixed = lambda i: (0, 0)
    out_bf = jax.ShapeDtypeStruct((m, FOX_WIDTH), BF16)
    pipelined = tm * D_MODEL * 4 + 3 * tm * FOX_WIDTH * 2 + tm * V7X_LANES * 4
    resident = D_MODEL * 4 + D_MODEL * 3 * FOX_WIDTH * 2 + D_MODEL * V7X_LANES * 2
    temps = tm * D_MODEL * (4 + 2) + 2 * tm * FOX_WIDTH * 4
    return pl.pallas_call(
        _fox_inproj_kernel,
        grid=(m // tm,),
        in_specs=[
            pl.BlockSpec((tm, D_MODEL), row),
            _resident((1, D_MODEL), fixed),
            _resident((D_MODEL, 2 * FOX_WIDTH), fixed),
            _resident((FOX_WIDTH, D_MODEL), fixed),
            _resident((D_MODEL, V7X_LANES), fixed),
        ],
        out_specs=[
            pl.BlockSpec((tm, FOX_WIDTH), row),
            pl.BlockSpec((tm, FOX_WIDTH), row),
            pl.BlockSpec((FOX_WIDTH, tm), lambda i: (0, i)),
            pl.BlockSpec((tm, V7X_LANES), row),
        ],
        out_shape=[out_bf, out_bf, jax.ShapeDtypeStruct((FOX_WIDTH, m), BF16),
                   jax.ShapeDtypeStruct((m, V7X_LANES), F32)],
        compiler_params=_compiler_params(1, pipelined, resident, temps),
        name="fox_inproj",
    )(x2d, g, w_qk, w_vt, w_f)


def _split3(x):
    hi = x.astype(BF16)
    rem = x - hi.astype(F32)
    mid = rem.astype(BF16)
    lo = (rem - mid.astype(F32)).astype(BF16)
    return hi, mid, lo


def _gate_cumsum_kernel(f_ref, b_ref, sel_ref, e_ref, carry_ref, *, sb):
    @pl.when(pl.program_id(1) == 0)
    def _():
        carry_ref[...] = jnp.zeros_like(carry_ref)

    z = f_ref[0] + b_ref[...]
    ls = jnp.minimum(z, 0.0) - jnp.log1p(jnp.exp(-jnp.abs(z)))
    row = lax.broadcasted_iota(jnp.int32, (sb, sb), 0)
    col = lax.broadcasted_iota(jnp.int32, (sb, sb), 1)
    tri = jnp.where(row >= col, 1.0, 0.0).astype(BF16)
    c = sum(_dot(tri, piece) for piece in _split3(ls)) + carry_ref[...]
    carry_ref[...] = c[sb - 1:sb, :]
    pieces = _split3(c * LOG2E)
    e_ref[0] = sum(_dot(pieces[i], sel_ref[i]) for i in range(3)).astype(BF16)


def _gate_select():
    sel = np.zeros((3, V7X_LANES, FOX_WIDTH), np.float32)
    for i in range(3):
        for h in range(FOX_HEADS):
            sel[i, h, V7X_LANES * (h // 2) + 3 * (h % 2) + i] = 1.0
    return jnp.asarray(sel, dtype=BF16)


def _gate_cumsum(f_logit, b_f):
    b, s, _ = f_logit.shape
    sb = SEQ_TILE
    pipelined = sb * V7X_LANES * 4 + sb * FOX_WIDTH * 2
    resident = V7X_LANES * 4 + 3 * V7X_LANES * FOX_WIDTH * 2
    temps = 2 * sb * sb * 4 + 8 * sb * V7X_LANES * 4 + 2 * sb * FOX_WIDTH * 4
    return pl.pallas_call(
        functools.partial(_gate_cumsum_kernel, sb=sb),
        grid=(b, s // sb),
        in_specs=[
            pl.BlockSpec((1, sb, V7X_LANES), lambda i, j: (i, j, 0)),
            _resident((1, V7X_LANES), lambda i, j: (0, 0)),
            _resident((3, V7X_LANES, FOX_WIDTH), lambda i, j: (0, 0, 0)),
        ],
        out_specs=pl.BlockSpec((1, sb, FOX_WIDTH), lambda i, j: (i, j, 0)),
        out_shape=jax.ShapeDtypeStruct((b, s, FOX_WIDTH), BF16),
        scratch_shapes=[pltpu.VMEM((1, V7X_LANES), F32)],
        compiler_params=_compiler_params(2, pipelined, resident, temps),
        name="gate_cumsum",
    )(f_logit, b_f, _gate_select())


def _causal_attention(qi, t, chains, score_fn, value_fn, s_scr, m_scr, l_scr, acc_scr):
    m_scr[...] = jnp.full(m_scr.shape, -jnp.inf, F32)
    l_scr[...] = jnp.zeros(l_scr.shape, F32)
    acc_scr[...] = jnp.zeros(acc_scr.shape, F32)

    def issue(j, slot):
        for c in range(chains):
            s_scr[slot, c] = score_fn(j, c)

    def consume(j, slot, keep):
        for c in range(chains):
            s_t = s_scr[slot, c]
            if keep is not None:
                s_t = jnp.where(keep, s_t, -jnp.inf)
            m = m_scr[c]
            m_new = jnp.maximum(m, jnp.max(s_t, axis=0, keepdims=True))
            alpha = jnp.exp2(m - m_new)
            p_t = jnp.exp2(s_t - m_new)
            m_scr[c] = m_new
            l_scr[c] = alpha * l_scr[c] + jnp.sum(p_t, axis=0, keepdims=True)
            acc_scr[c] = alpha * acc_scr[c] + _dot(value_fn(j, c), p_t.astype(BF16))

    issue(0, 0)

    def two_blocks(i, carry):
        j = 2 * i
        issue(j + 1, 1)
        consume(j, 0, None)
        issue(j + 2, 0)
        consume(j + 1, 1, None)
        return carry

    lax.fori_loop(0, qi // 2, two_blocks, 0)

    @pl.when(qi % 2 == 1)
    def _():
        issue(qi, 1)
        consume(qi - 1, 0, None)

    key = lax.broadcasted_iota(jnp.int32, (t, t), 0)
    query = lax.broadcasted_iota(jnp.int32, (t, t), 1)
    consume(qi, qi % 2, key <= query)


def _attention_scratch(t, chains, width):
    return [pltpu.VMEM((2, chains, t, t), F32), pltpu.VMEM((chains, 1, t), F32),
            pltpu.VMEM((chains, 1, t), F32), pltpu.VMEM((chains, width, t), F32)]


def _attention_scratch_bytes(t, chains, width):
    return 2 * chains * t * t * 4 + chains * (2 * 8 + width) * t * 4


def _fox_attn_kernel(q_ref, k_ref, e_ref, vt_ref, o_ref, s_scr, m_scr, l_scr, acc_scr, *, t):
    q = q_ref[0]
    lane = lax.broadcasted_iota(jnp.int32, (t, V7X_LANES), 1)
    zero = jnp.zeros_like(q)
    q_head = []
    for h in range(2):
        own = (lane >= h * FOX_HEAD_DIM) & (lane < (h + 1) * FOX_HEAD_DIM)
        minus_one = jnp.where((lane >= 3 * h) & (lane < 3 * h + 3), -1.0, 0.0).astype(BF16)
        q_head.append(jnp.concatenate([jnp.where(own, q, zero), minus_one], axis=1))

    def score_fn(j, h):
        ks = pl.multiple_of(j * t, t)
        k = jnp.concatenate([k_ref[0, pl.ds(ks, t), :], e_ref[0, pl.ds(ks, t), :]], axis=1)
        return lax.dot_general(k, q_head[h], _NT, preferred_element_type=F32)

    def value_fn(j, h):
        rows = slice(h * FOX_HEAD_DIM, (h + 1) * FOX_HEAD_DIM)
        return vt_ref[rows, pl.ds(pl.multiple_of(j * t, t), t)]

    _causal_attention(pl.program_id(2), t, 2, score_fn, value_fn, s_scr, m_scr, l_scr, acc_scr)
    out_t = jnp.concatenate([acc_scr[h] / l_scr[h] for h in range(2)], axis=0)
    o_ref[0] = out_t.T.astype(o_ref.dtype)


def _fox_attn(q, k, ext, v_t):
    b, s, _ = q.shape
    t = SEQ_TILE
    pairs = FOX_HEADS // 2
    pipelined = 2 * t * V7X_LANES * 2 + 3 * s * V7X_LANES * 2
    temps = _attention_scratch_bytes(t, 2, FOX_HEAD_DIM) + 2 * (t * t * (4 + 2) + t * V7X_LANES * 4)
    return pl.pallas_call(
        functools.partial(_fox_attn_kernel, t=t),
        grid=(b, pairs, s // t),
        in_specs=[
            pl.BlockSpec((1, t, V7X_LANES), lambda i, p, j: (i, j, p)),
            pl.BlockSpec((1, s, V7X_LANES), lambda i, p, j: (i, 0, p)),
            pl.BlockSpec((1, s, V7X_LANES), lambda i, p, j: (i, 0, p)),
            pl.BlockSpec((V7X_LANES, s), lambda i, p, j: (p, i)),
        ],
        out_specs=pl.BlockSpec((1, t, V7X_LANES), lambda i, p, j: (i, j, p)),
        out_shape=jax.ShapeDtypeStruct((b, s, FOX_WIDTH), BF16),
        scratch_shapes=_attention_scratch(t, 2, FOX_HEAD_DIM),
        compiler_params=_compiler_params(3, pipelined, 0, temps),
        name="fox_attn",
    )(q, k, ext, v_t)


def _mix_ffn_kernel(x_ref, ctx_ref, wo_ref, g_ref, wu_ref, wd_ref, gf_ref, o_ref, *, final_norm):
    x1 = x_ref[...] + _dot(ctx_ref[...], wo_ref[...])
    hn = (_rms_unit(x1) * g_ref[...]).astype(BF16)
    acc = x1
    for c in range(D_FF // FF_CHUNK):
        cols = slice(c * FF_CHUNK, (c + 1) * FF_CHUNK)
        up = jnp.maximum(_dot(hn, wu_ref[:, cols]), 0.0)
        acc = acc + _dot((up * up).astype(BF16), wd_ref[cols, :])
    if final_norm:
        acc = _rms_unit(acc) * gf_ref[...]
    o_ref[...] = acc


def _mix_ffn(x2d, ctx2d, w_out, g_ffn, w_up, w_down, g_final, final_norm):
    m = x2d.shape[0]
    tm = ROW_TILE
    row = lambda i: (i, 0)
    fixed = lambda i: (0, 0)
    pipelined = tm * D_MODEL * (4 + 2 + 4)
    resident = (D_MODEL * D_MODEL + 2 * D_MODEL * D_FF) * 2 + 2 * D_MODEL * 4
    temps = tm * D_MODEL * (4 + 4 + 2) + tm * FF_CHUNK * (4 + 4 + 2)
    return pl.pallas_call(
        functools.partial(_mix_ffn_kernel, final_norm=final_norm),
        grid=(m // tm,),
        in_specs=[
            pl.BlockSpec((tm, D_MODEL), row),
            pl.BlockSpec((tm, D_MODEL), row),
            _resident((D_MODEL, D_MODEL), fixed),
            _resident((1, D_MODEL), fixed),
            _resident((D_MODEL, D_FF), fixed),
            _resident((D_FF, D_MODEL), fixed),
            _resident((1, D_MODEL), fixed),
        ],
        out_specs=pl.BlockSpec((tm, D_MODEL), row),
        out_shape=jax.ShapeDtypeStruct((m, D_MODEL), F32),
        compiler_params=_compiler_params(1, pipelined, resident, temps),
        name="mix_ffn_final" if final_norm else "mix_ffn",
    )(x2d, ctx2d, w_out, g_ffn, w_up, w_down, g_final)


def _rope_lanes(t, cos, sin_signed):
    lane = lax.broadcasted_iota(jnp.int32, t.shape, 1)
    low = (lane % QK_ROPE_DIM) < (QK_ROPE_DIM // 2)
    partner = jnp.where(low,
                        pltpu.roll(t, V7X_LANES - QK_ROPE_DIM // 2, 1),
                        pltpu.roll(t, QK_ROPE_DIM // 2, 1))
    return t * cos + partner * sin_signed


def _mla_proj_kernel(x_ref, gkv_ref, gq_ref, wkva_ref, gkva_ref, wkvb_ref, wvt_ref, wqa_ref,
                     gqa_ref, wqb_ref, cos_ref, sin_ref, qcat_ref, kcat_ref, vt_ref):
    unit = _rms_unit(x_ref[...])
    cos = cos_ref[...]
    sin = sin_ref[...]
    tm = unit.shape[0]
    first = lax.broadcasted_iota(jnp.int32, (tm, V7X_LANES), 1) < QK_ROPE_DIM

    src = (unit * gkv_ref[...]).astype(BF16)
    kv_a = _dot(src, wkva_ref[...])
    c_kv = (_rms_unit(kv_a[:, 0:KV_LORA_RANK]) * gkva_ref[...]).astype(BF16)
    k_rope = _rope_lanes(kv_a[:, KV_LORA_RANK:KV_LORA_RANK + V7X_LANES], cos, sin)
    k_rope_even = k_rope.astype(BF16)
    k_rope_odd = pltpu.roll(k_rope, QK_ROPE_DIM, 1).astype(BF16)
    k_nope = _dot(c_kv, wkvb_ref[...]).astype(BF16)
    vt_ref[...] = lax.dot_general(wvt_ref[...], c_kv, _NT, preferred_element_type=F32).astype(BF16)

    h = (unit * gq_ref[...]).astype(BF16)
    c_q = (_rms_unit(_dot(h, wqa_ref[...])) * gqa_ref[...]).astype(BF16)
    q_nope = (_dot(c_q, wqb_ref[:, 0:MLA_HEADS * QK_NOPE_DIM]) * MLA_QSCALE).astype(BF16)
    q_rope_all = _dot(c_q, wqb_ref[:, MLA_HEADS * QK_NOPE_DIM:]) * MLA_QSCALE
    zero = jnp.zeros((tm, V7X_LANES), BF16)

    for pair in range(MLA_HEADS // 2):
        lanes = slice(pair * V7X_LANES, (pair + 1) * V7X_LANES)
        q_rope = _rope_lanes(q_rope_all[:, lanes], cos, sin).astype(BF16)
        for odd in range(2):
            head = 2 * pair + odd
            nope = slice(head * QK_NOPE_DIM, (head + 1) * QK_NOPE_DIM)
            lo = head * MLA_QK_PAD
            qcat_ref[:, lo:lo + QK_NOPE_DIM] = q_nope[:, nope]
            kcat_ref[:, lo:lo + QK_NOPE_DIM] = k_nope[:, nope]
            if odd:
                qcat_ref[:, lo + QK_NOPE_DIM:lo + MLA_QK_PAD] = jnp.where(first, zero, q_rope)
                kcat_ref[:, lo + QK_NOPE_DIM:lo + MLA_QK_PAD] = k_rope_odd
            else:
                qcat_ref[:, lo + QK_NOPE_DIM:lo + MLA_QK_PAD] = jnp.where(first, q_rope, zero)
                kcat_ref[:, lo + QK_NOPE_DIM:lo + MLA_QK_PAD] = k_rope_even


def _mla_proj(x2d, g_kv, g_q, w_kv_a, g_kv_a, w_k_b, w_vt_b, w_q_a, g_q_a, w_q_b, cos_t, sin_t,
              seq):
    m = x2d.shape[0]
    tm = ROW_TILE
    row = lambda i: (i, 0)
    fixed = lambda i: (0, 0)
    pos = lambda i: (i % (seq // tm), 0)
    kva_cols = w_kv_a.shape[1]
    cat = MLA_HEADS * MLA_QK_PAD
    pipelined = tm * D_MODEL * 4 + 2 * tm * V7X_LANES * 4 + tm * (2 * cat + D_MODEL) * 2
    resident = (D_MODEL * kva_cols + KV_LORA_RANK * 2 * D_MODEL + D_MODEL * Q_LORA_RANK
                + Q_LORA_RANK * 3 * D_MODEL // 2) * 2 + 4 * D_MODEL * 4
    temps = tm * D_MODEL * (4 + 2 + 2) + 4 * tm * D_MODEL * 4
    return pl.pallas_call(
        _mla_proj_kernel,
        grid=(m // tm,),
        in_specs=[
            pl.BlockSpec((tm, D_MODEL), row),
            _resident((1, D_MODEL), fixed),
            _resident((1, D_MODEL), fixed),
            _resident((D_MODEL, kva_cols), fixed),
            _resident((1, KV_LORA_RANK), fixed),
            _resident((KV_LORA_RANK, MLA_HEADS * QK_NOPE_DIM), fixed),
            _resident((MLA_HEADS * V_HEAD_DIM, KV_LORA_RANK), fixed),
            _resident((D_MODEL, Q_LORA_RANK), fixed),
            _resident((1, Q_LORA_RANK), fixed),
            _resident((Q_LORA_RANK, 3 * D_MODEL // 2), fixed),
            pl.BlockSpec((tm, V7X_LANES), pos),
            pl.BlockSpec((tm, V7X_LANES), pos),
        ],
        out_specs=[
            pl.BlockSpec((tm, cat), row),
            pl.BlockSpec((tm, cat), row),
            pl.BlockSpec((MLA_HEADS * V_HEAD_DIM, tm), lambda i: (0, i)),
        ],
        out_shape=[
            jax.ShapeDtypeStruct((m, cat), BF16),
            jax.ShapeDtypeStruct((m, cat), BF16),
            jax.ShapeDtypeStruct((MLA_HEADS * V_HEAD_DIM, m), BF16),
        ],
        compiler_params=_compiler_params(1, pipelined, resident, temps),
        name="mla_proj",
    )(x2d, g_kv, g_q, w_kv_a, g_kv_a, w_k_b, w_vt_b, w_q_a, g_q_a, w_q_b, cos_t, sin_t)


MLA_HEADS_PER_STEP = 2


def _mla_attn_kernel(q_ref, k_ref, vt_ref, o_ref, s_scr, m_scr, l_scr, acc_scr, *, t):
    n = MLA_HEADS_PER_STEP
    q_head = [q_ref[0, :, h * MLA_QK_PAD:(h + 1) * MLA_QK_PAD] for h in range(n)]

    def score_fn(j, h):
        k = k_ref[0, pl.ds(pl.multiple_of(j * t, t), t), h * MLA_QK_PAD:(h + 1) * MLA_QK_PAD]
        return lax.dot_general(k, q_head[h], _NT, preferred_element_type=F32)

    def value_fn(j, h):
        rows = slice(h * V_HEAD_DIM, (h + 1) * V_HEAD_DIM)
        return vt_ref[rows, pl.ds(pl.multiple_of(j * t, t), t)]

    _causal_attention(pl.program_id(2), t, n, score_fn, value_fn, s_scr, m_scr, l_scr, acc_scr)
    out_t = jnp.concatenate([acc_scr[h] / l_scr[h] for h in range(n)], axis=0)
    o_ref[0] = out_t.T.astype(o_ref.dtype)


def _mla_attn(q_cat, k_cat, v_t, b, s):
    t = SEQ_TILE
    n = MLA_HEADS_PER_STEP
    pipelined = n * (t * MLA_QK_PAD * 2 + s * MLA_QK_PAD * 2 + s * V_HEAD_DIM * 2
                     + t * V_HEAD_DIM * 2)
    temps = _attention_scratch_bytes(t, n, V_HEAD_DIM) + n * (t * t * (4 + 2) + t * V_HEAD_DIM * 4)
    return pl.pallas_call(
        functools.partial(_mla_attn_kernel, t=t),
        grid=(b, MLA_HEADS // n, s // t),
        in_specs=[
            pl.BlockSpec((1, t, n * MLA_QK_PAD), lambda i, h, j: (i, j, h)),
            pl.BlockSpec((1, s, n * MLA_QK_PAD), lambda i, h, j: (i, 0, h)),
            pl.BlockSpec((n * V_HEAD_DIM, s), lambda i, h, j: (h, i)),
        ],
        out_specs=pl.BlockSpec((1, t, n * V_HEAD_DIM), lambda i, h, j: (i, j, h)),
        out_shape=jax.ShapeDtypeStruct((b, s, MLA_HEADS * V_HEAD_DIM), BF16),
        scratch_shapes=_attention_scratch(t, n, V_HEAD_DIM),
        compiler_params=_compiler_params(3, pipelined, 0, temps),
        name="mla_attn",
    )(q_cat, k_cat, v_t)


def _rope_tables(seq):
    half = QK_ROPE_DIM // 2
    inv = 1.0 / (ROPE_BASE ** (jnp.arange(0, QK_ROPE_DIM, 2, dtype=F32) / QK_ROPE_DIM))
    ang = jnp.arange(seq, dtype=F32)[:, None] * inv[None, :]
    cos, sin = jnp.cos(ang), jnp.sin(ang)
    reps = V7X_LANES // half
    cos_t = jnp.tile(cos, (1, reps))
    sin_t = jnp.tile(jnp.concatenate([-sin, sin], axis=1), (1, reps // 2))
    return cos_t, sin_t


def _row(vec):
    return vec.reshape(1, -1).astype(F32)


def kernel(x, norm_mix_g, norm_ffn_g, fox_w_in, fox_b_f, fox_w_out, kv_norm_g, mla_w_kv_a,
           mla_kv_a_norm_g, mla_w_kv_b, mla_w_q_a, mla_q_a_norm_g, mla_w_q_b, mla_w_out,
           ffn_w_up, ffn_w_down, final_norm_g):
    b, s, d = x.shape
    assert d == D_MODEL and s % SEQ_TILE == 0 and (b * s) % ROW_TILE == 0
    assert fox_w_in.shape[0] == 1 and mla_w_q_a.shape[0] == 1
    x2d = x.reshape(b * s, d)

    w_in = fox_w_in[0]
    w_qk = w_in[:, :2 * FOX_WIDTH].astype(BF16)
    w_vt = w_in[:, 2 * FOX_WIDTH:3 * FOX_WIDTH].T.astype(BF16)
    w_f = jnp.pad(w_in[:, 3 * FOX_WIDTH:], ((0, 0), (0, V7X_LANES - FOX_HEADS))).astype(BF16)
    b_f = jnp.pad(fox_b_f[0].astype(F32), (0, V7X_LANES - FOX_HEADS)).reshape(1, V7X_LANES)
    q, k, v_t, f_logit = _fox_inproj(x2d, _row(norm_mix_g[0]), w_qk, w_vt, w_f)
    ext = _gate_cumsum(f_logit.reshape(b, s, V7X_LANES), b_f)
    ctx = _fox_attn(q.reshape(b, s, FOX_WIDTH), k.reshape(b, s, FOX_WIDTH), ext, v_t)
    x2d = _mix_ffn(x2d, ctx.reshape(b * s, FOX_WIDTH), fox_w_out[0].astype(BF16),
                   _row(norm_ffn_g[0]), ffn_w_up[0].astype(BF16), ffn_w_down[0].astype(BF16),
                   _row(final_norm_g), final_norm=False)

    w_kv_a = jnp.pad(mla_w_kv_a, ((0, 0), (0, V7X_LANES - QK_ROPE_DIM))).astype(BF16)
    w_kv_b = mla_w_kv_b.reshape(KV_LORA_RANK, MLA_HEADS, QK_NOPE_DIM + V_HEAD_DIM)
    w_k_b = w_kv_b[:, :, :QK_NOPE_DIM].reshape(KV_LORA_RANK, -1).astype(BF16)
    w_vt_b = w_kv_b[:, :, QK_NOPE_DIM:].reshape(KV_LORA_RANK, -1).T.astype(BF16)
    w_q_b = mla_w_q_b[0].reshape(Q_LORA_RANK, MLA_HEADS, QK_NOPE_DIM + QK_ROPE_DIM)
    w_q_b = jnp.concatenate(
        [w_q_b[:, :, :QK_NOPE_DIM].reshape(Q_LORA_RANK, -1),
         w_q_b[:, :, QK_NOPE_DIM:].reshape(Q_LORA_RANK, -1)], axis=1).astype(BF16)
    cos_t, sin_t = _rope_tables(s)
    q_cat, k_cat, v_t = _mla_proj(
        x2d, _row(kv_norm_g), _row(norm_mix_g[1]), w_kv_a, _row(mla_kv_a_norm_g), w_k_b, w_vt_b,
        mla_w_q_a[0].astype(BF16), _row(mla_q_a_norm_g[0]), w_q_b, cos_t, sin_t, s)

    cat = MLA_HEADS * MLA_QK_PAD
    ctx = _mla_attn(q_cat.reshape(b, s, cat), k_cat.reshape(b, s, cat), v_t, b, s)
    out = _mix_ffn(x2d, ctx.reshape(b * s, MLA_HEADS * V_HEAD_DIM), mla_w_out[0].astype(BF16),
                   _row(norm_ffn_g[1]), ffn_w_up[1].astype(BF16), ffn_w_down[1].astype(BF16),
                   _row(final_norm_g), final_norm=True)
    return out.reshape(b, s, d)
```

```python
import functools
import math

import jax
import jax.numpy as jnp
import numpy as np
from jax import lax
from jax.experimental import pallas as pl
from jax.experimental.pallas import tpu as pltpu

D_MODEL = 1024
FOX_HEADS = 16
FOX_HEAD_DIM = 64
FOX_WIDTH = FOX_HEADS * FOX_HEAD_DIM
MLA_HEADS = 8
QK_NOPE_DIM = 128
QK_ROPE_DIM = 64
V_HEAD_DIM = 128
Q_LORA_RANK = 384
KV_LORA_RANK = 256
ROPE_BASE = 10000.0
D_FF = 4 * D_MODEL
EPS = 1e-6

V7X_LANES = 128
V7X_VMEM_BYTES = 64 * 1024 * 1024
BF16_SUBLANES = 16
V7X_MXU_WIDTH = 256

LOG2E = 1.4426950408889634
FOX_QSCALE = (FOX_HEAD_DIM ** -0.5) * LOG2E
MLA_QSCALE = ((QK_NOPE_DIM + QK_ROPE_DIM) ** -0.5) * LOG2E
MLA_QK_PAD = 256

ROW_TILE = 512
SEQ_TILE = 512
ATTN_Q_TILE = 1024
FF_CHUNK = 512

BF16 = jnp.bfloat16
F32 = jnp.float32
_NT = (((1,), (1,)), ((), ()))


def _compiler_params(n_grid, pipelined_bytes, resident_bytes, temp_bytes, flags=None):
    need = 2 * pipelined_bytes + resident_bytes + temp_bytes
    return pltpu.CompilerParams(
        dimension_semantics=("arbitrary",) * n_grid,
        vmem_limit_bytes=min(need, V7X_VMEM_BYTES),
        flags=flags,
    )


ATTN_FLAGS = None


def _resident(shape, index_map):
    return pl.BlockSpec(shape, index_map, pipeline_mode=pl.Buffered(1))


def _rms_unit(xf):
    return xf * lax.rsqrt(jnp.mean(xf * xf, axis=-1, keepdims=True) + EPS)


def _dot(a, b):
    return jnp.dot(a, b, preferred_element_type=F32)


def _fox_inproj_kernel(x_ref, g_ref, wk_ref, wqvt_ref, wf_ref, qt_ref, k_ref, vt_ref, f_ref):
    h = (_rms_unit(x_ref[...]) * g_ref[...]).astype(BF16)
    k_ref[...] = _dot(h, wk_ref[...]).astype(BF16)
    q_t = lax.dot_general(wqvt_ref[0:FOX_WIDTH, :], h, _NT, preferred_element_type=F32)
    qt_ref[...] = (q_t * FOX_QSCALE).astype(BF16)
    v_t = lax.dot_general(wqvt_ref[FOX_WIDTH:2 * FOX_WIDTH, :], h, _NT, preferred_element_type=F32)
    vt_ref[...] = v_t.astype(BF16)
    f_ref[...] = _dot(h, wf_ref[...])


def _fox_inproj(x2d, g, w_k, w_qvt, w_f):
    m = x2d.shape[0]
    tm = ROW_TILE
    row = lambda i: (i, 0)
    col = lambda i: (0, i)
    fixed = lambda i: (0, 0)
    feature_major = jax.ShapeDtypeStruct((FOX_WIDTH, m), BF16)
    pipelined = tm * D_MODEL * 4 + 3 * tm * FOX_WIDTH * 2 + tm * V7X_LANES * 4
    resident = D_MODEL * 4 + D_MODEL * 3 * FOX_WIDTH * 2 + D_MODEL * V7X_LANES * 2
    temps = tm * D_MODEL * (4 + 2) + 2 * tm * FOX_WIDTH * 4
    return pl.pallas_call(
        _fox_inproj_kernel,
        grid=(m // tm,),
        in_specs=[
            pl.BlockSpec((tm, D_MODEL), row),
            _resident((1, D_MODEL), fixed),
            _resident((D_MODEL, FOX_WIDTH), fixed),
            _resident((2 * FOX_WIDTH, D_MODEL), fixed),
            _resident((D_MODEL, V7X_LANES), fixed),
        ],
        out_specs=[
            pl.BlockSpec((FOX_WIDTH, tm), col),
            pl.BlockSpec((tm, FOX_WIDTH), row),
            pl.BlockSpec((FOX_WIDTH, tm), col),
            pl.BlockSpec((tm, V7X_LANES), row),
        ],
        out_shape=[feature_major, jax.ShapeDtypeStruct((m, FOX_WIDTH), BF16), feature_major,
                   jax.ShapeDtypeStruct((m, V7X_LANES), F32)],
        compiler_params=_compiler_params(1, pipelined, resident, temps),
        name="fox_inproj",
    )(x2d, g, w_k, w_qvt, w_f)


def _split3(x):
    hi = x.astype(BF16)
    rem = x - hi.astype(F32)
    mid = rem.astype(BF16)
    lo = (rem - mid.astype(F32)).astype(BF16)
    return hi, mid, lo


def _gate_cumsum_kernel(f_ref, b_ref, sel_ref, e_ref, carry_ref, *, sb):
    @pl.when(pl.program_id(1) == 0)
    def _():
        carry_ref[...] = jnp.zeros_like(carry_ref)

    z = f_ref[0] + b_ref[...]
    ls = jnp.minimum(z, 0.0) - jnp.log1p(jnp.exp(-jnp.abs(z)))
    row = lax.broadcasted_iota(jnp.int32, (sb, sb), 0)
    col = lax.broadcasted_iota(jnp.int32, (sb, sb), 1)
    tri = jnp.where(row >= col, 1.0, 0.0).astype(BF16)
    c = sum(_dot(tri, piece) for piece in _split3(ls)) + carry_ref[...]
    carry_ref[...] = c[sb - 1:sb, :]
    pieces = _split3(c * LOG2E)
    e_ref[0] = sum(_dot(pieces[i], sel_ref[i]) for i in range(3)).astype(BF16)


def _gate_select():
    sel = np.zeros((3, V7X_LANES, FOX_WIDTH), np.float32)
    for i in range(3):
        for h in range(FOX_HEADS):
            sel[i, h, V7X_LANES * (h // 2) + 3 * (h % 2) + i] = 1.0
    return jnp.asarray(sel, dtype=BF16)


def _gate_cumsum(f_logit, b_f):
    b, s, _ = f_logit.shape
    sb = SEQ_TILE
    pipelined = sb * V7X_LANES * 4 + sb * FOX_WIDTH * 2
    resident = V7X_LANES * 4 + 3 * V7X_LANES * FOX_WIDTH * 2
    temps = 2 * sb * sb * 4 + 8 * sb * V7X_LANES * 4 + 2 * sb * FOX_WIDTH * 4
    return pl.pallas_call(
        functools.partial(_gate_cumsum_kernel, sb=sb),
        grid=(b, s // sb),
        in_specs=[
            pl.BlockSpec((1, sb, V7X_LANES), lambda i, j: (i, j, 0)),
            _resident((1, V7X_LANES), lambda i, j: (0, 0)),
            _resident((3, V7X_LANES, FOX_WIDTH), lambda i, j: (0, 0, 0)),
        ],
        out_specs=pl.BlockSpec((1, sb, FOX_WIDTH), lambda i, j: (i, j, 0)),
        out_shape=jax.ShapeDtypeStruct((b, s, FOX_WIDTH), BF16),
        scratch_shapes=[pltpu.VMEM((1, V7X_LANES), F32)],
        compiler_params=_compiler_params(2, pipelined, resident, temps),
        name="gate_cumsum",
    )(f_logit, b_f, _gate_select())


def _causal_attention(qi, chains, score_fn, value_fn, s_scr, bmax_scr, m_scr, acc_scr):
    _, _, tk, tq = s_scr.shape
    assert tq == 2 * tk
    strips = tq // V7X_MXU_WIDTH
    m_scr[...] = jnp.full(m_scr.shape, -jnp.inf, F32)
    acc_scr[...] = jnp.zeros(acc_scr.shape, F32)
    ones = jnp.ones((BF16_SUBLANES, tk), BF16)

    def lanes(n):
        return slice(n * V7X_MXU_WIDTH, (n + 1) * V7X_MXU_WIDTH)

    def issue(j, slot, c, n):
        s_t = score_fn(j, c, n)
        s_scr[slot, c, :, lanes(n)] = s_t
        bmax_scr[slot, c, :, lanes(n)] = jnp.max(s_t, axis=0, keepdims=True)

    def consume(j, slot, c, n, key_offset=None):
        s_t = s_scr[slot, c, :, lanes(n)]
        if key_offset is None:
            block_max = bmax_scr[slot, c, :, lanes(n)]
        else:
            key = lax.broadcasted_iota(jnp.int32, s_t.shape, 0) + key_offset
            query = lax.broadcasted_iota(jnp.int32, s_t.shape, 1) + n * V7X_MXU_WIDTH
            s_t = jnp.where(key <= query, s_t, -jnp.inf)
            block_max = jnp.max(s_t, axis=0, keepdims=True)
        m = m_scr[c, :, lanes(n)]
        m_new = jnp.maximum(m, block_max)
        p_t = jnp.exp2(s_t - m_new).astype(BF16)
        m_scr[c, :, lanes(n)] = m_new
        v_ones = jnp.concatenate([value_fn(j, c), ones], axis=0)
        acc_scr[c, :, lanes(n)] = (jnp.exp2(m - m_new) * acc_scr[c, :, lanes(n)]
                                   + _dot(v_ones, p_t))

    units = [(c, n) for n in range(strips) for c in range(chains)]

    def issue_block(j, slot):
        for c, n in units:
            issue(j, slot, c, n)

    def consume_and_issue(j, slot):
        for c, n in units:
            consume(j, slot, c, n)
            issue(j + 1, 1 - slot, c, n)

    issue_block(0, 0)

    def two_blocks(i, carry):
        consume_and_issue(2 * i, 0)
        consume_and_issue(2 * i + 1, 1)
        return carry

    lax.fori_loop(0, qi, two_blocks, 0)

    half = strips // 2
    for c, n in units:
        consume(2 * qi, 0, c, n, key_offset=0 if n < half else None)
        if n >= half:
            issue(2 * qi + 1, 1, c, n)
    for c, n in units:
        if n >= half:
            consume(2 * qi + 1, 1, c, n, key_offset=tk)


def _attention_scratch(tq, chains, width):
    tk = tq // 2
    return [pltpu.VMEM((2, chains, tk, tq), F32), pltpu.VMEM((2, chains, 1, tq), F32),
            pltpu.VMEM((chains, 1, tq), F32),
            pltpu.VMEM((chains, width + BF16_SUBLANES, tq), F32)]


def _attention_scratch_bytes(tq, chains, width):
    return chains * tq * tq * 4 + chains * (3 * 8 + width + BF16_SUBLANES) * tq * 4


def _attention_output(acc_scr, chains, width):
    return jnp.concatenate(
        [acc_scr[c, 0:width, :] / acc_scr[c, width:width + 1, :] for c in range(chains)], axis=0)


def _fox_attn_kernel(qt_ref, k_ref, e_ref, vt_ref, o_ref, s_scr, bmax_scr, m_scr, acc_scr):
    tk = s_scr.shape[2]
    q_t = qt_ref[...]
    row = lax.broadcasted_iota(jnp.int32, q_t.shape, 0)
    zero = jnp.zeros_like(q_t)
    q_head = []
    for h in range(2):
        own = (row >= h * FOX_HEAD_DIM) & (row < (h + 1) * FOX_HEAD_DIM)
        minus_one = jnp.where((row >= 3 * h) & (row < 3 * h + 3), -1.0, 0.0).astype(BF16)
        q_head.append(jnp.concatenate([jnp.where(own, q_t, zero), minus_one], axis=0))

    def score_fn(j, h, n):
        ks = pl.multiple_of(j * tk, tk)
        k = jnp.concatenate([k_ref[0, pl.ds(ks, tk), :], e_ref[0, pl.ds(ks, tk), :]], axis=1)
        return _dot(k, q_head[h][:, n * V7X_MXU_WIDTH:(n + 1) * V7X_MXU_WIDTH])

    def value_fn(j, h):
        rows = slice(h * FOX_HEAD_DIM, (h + 1) * FOX_HEAD_DIM)
        return vt_ref[rows, pl.ds(pl.multiple_of(j * tk, tk), tk)]

    _causal_attention(pl.program_id(2), 2, score_fn, value_fn, s_scr, bmax_scr, m_scr, acc_scr)
    o_ref[0] = _attention_output(acc_scr, 2, FOX_HEAD_DIM).T.astype(o_ref.dtype)


def _fox_attn(q_t, k, ext, v_t):
    b, s, _ = k.shape
    t = ATTN_Q_TILE
    pairs = FOX_HEADS // 2
    pipelined = 2 * t * V7X_LANES * 2 + 3 * s * V7X_LANES * 2
    temps = (_attention_scratch_bytes(t, 2, FOX_HEAD_DIM)
             + 2 * (t * t // 2 * (4 + 2) + t * V7X_LANES * 4))
    return pl.pallas_call(
        _fox_attn_kernel,
        grid=(b, pairs, s // t),
        in_specs=[
            pl.BlockSpec((V7X_LANES, t), lambda i, p, j: (p, i * (s // t) + j)),
            pl.BlockSpec((1, s, V7X_LANES), lambda i, p, j: (i, 0, p)),
            pl.BlockSpec((1, s, V7X_LANES), lambda i, p, j: (i, 0, p)),
            pl.BlockSpec((V7X_LANES, s), lambda i, p, j: (p, i)),
        ],
        out_specs=pl.BlockSpec((1, t, V7X_LANES), lambda i, p, j: (i, j, p)),
        out_shape=jax.ShapeDtypeStruct((b, s, FOX_WIDTH), BF16),
        scratch_shapes=_attention_scratch(t, 2, FOX_HEAD_DIM),
        compiler_params=_compiler_params(3, pipelined, 0, temps, ATTN_FLAGS),
        name="fox_attn",
    )(q_t, k, ext, v_t)


def _mix_ffn_kernel(x_ref, ctx_ref, wo_ref, g_ref, wu_ref, wd_ref, gf_ref, o_ref, *, final_norm):
    x1 = x_ref[...] + _dot(ctx_ref[...], wo_ref[...])
    hn = (_rms_unit(x1) * g_ref[...]).astype(BF16)
    acc = x1
    for c in range(D_FF // FF_CHUNK):
        cols = slice(c * FF_CHUNK, (c + 1) * FF_CHUNK)
        up = jnp.maximum(_dot(hn, wu_ref[:, cols]), 0.0)
        acc = acc + _dot((up * up).astype(BF16), wd_ref[cols, :])
    if final_norm:
        acc = _rms_unit(acc) * gf_ref[...]
    o_ref[...] = acc


def _mix_ffn(x2d, ctx2d, w_out, g_ffn, w_up, w_down, g_final, final_norm):
    m = x2d.shape[0]
    tm = ROW_TILE
    row = lambda i: (i, 0)
    fixed = lambda i: (0, 0)
    pipelined = tm * D_MODEL * (4 + 2 + 4)
    resident = (D_MODEL * D_MODEL + 2 * D_MODEL * D_FF) * 2 + 2 * D_MODEL * 4
    temps = tm * D_MODEL * (4 + 4 + 2) + tm * FF_CHUNK * (4 + 4 + 2)
    return pl.pallas_call(
        functools.partial(_mix_ffn_kernel, final_norm=final_norm),
        grid=(m // tm,),
        in_specs=[
            pl.BlockSpec((tm, D_MODEL), row),
            pl.BlockSpec((tm, D_MODEL), row),
            _resident((D_MODEL, D_MODEL), fixed),
            _resident((1, D_MODEL), fixed),
            _resident((D_MODEL, D_FF), fixed),
            _resident((D_FF, D_MODEL), fixed),
            _resident((1, D_MODEL), fixed),
        ],
        out_specs=pl.BlockSpec((tm, D_MODEL), row),
        out_shape=jax.ShapeDtypeStruct((m, D_MODEL), F32),
        compiler_params=_compiler_params(1, pipelined, resident, temps),
        name="mix_ffn_final" if final_norm else "mix_ffn",
    )(x2d, ctx2d, w_out, g_ffn, w_up, w_down, g_final)


def _rope_lanes(t, cos, sin_signed):
    lane = lax.broadcasted_iota(jnp.int32, t.shape, 1)
    low = (lane % QK_ROPE_DIM) < (QK_ROPE_DIM // 2)
    partner = jnp.where(low,
                        pltpu.roll(t, V7X_LANES - QK_ROPE_DIM // 2, 1),
                        pltpu.roll(t, QK_ROPE_DIM // 2, 1))
    return t * cos + partner * sin_signed


def _rope_rows(t, cos_t, sin_signed_t):
    half = QK_ROPE_DIM // 2
    partner = jnp.concatenate([t[half:, :], t[:half, :]], axis=0)
    return t * cos_t + partner * sin_signed_t


def _mla_proj_kernel(x_ref, gkv_ref, gq_ref, wkva_ref, gkva_ref, wkb_ref, wvt_ref, wqa_ref,
                     gqa_ref, wqt_ref, cos_ref, sin_ref, cost_ref, sint_ref,
                     qcat_t_ref, kcat_ref, vt_ref):
    unit = _rms_unit(x_ref[...])
    tm = unit.shape[0]
    nope_all = MLA_HEADS * QK_NOPE_DIM

    src = (unit * gkv_ref[...]).astype(BF16)
    kv_a = _dot(src, wkva_ref[...])
    c_kv = (_rms_unit(kv_a[:, 0:KV_LORA_RANK]) * gkva_ref[...]).astype(BF16)
    k_rope = _rope_lanes(kv_a[:, KV_LORA_RANK:KV_LORA_RANK + V7X_LANES],
                         cos_ref[...], sin_ref[...])
    k_rope_even = k_rope.astype(BF16)
    k_rope_odd = pltpu.roll(k_rope, QK_ROPE_DIM, 1).astype(BF16)
    k_nope = _dot(c_kv, wkb_ref[...]).astype(BF16)
    vt_ref[...] = lax.dot_general(wvt_ref[...], c_kv, _NT, preferred_element_type=F32).astype(BF16)
    for head in range(MLA_HEADS):
        lo = head * MLA_QK_PAD
        kcat_ref[:, lo:lo + QK_NOPE_DIM] = k_nope[:, head * QK_NOPE_DIM:(head + 1) * QK_NOPE_DIM]
        kcat_ref[:, lo + QK_NOPE_DIM:lo + MLA_QK_PAD] = k_rope_odd if head % 2 else k_rope_even

    h = (unit * gq_ref[...]).astype(BF16)
    c_q = (_rms_unit(_dot(h, wqa_ref[...])) * gqa_ref[...]).astype(BF16)
    q_nope_t = lax.dot_general(wqt_ref[0:nope_all, :], c_q, _NT, preferred_element_type=F32)
    q_rope_t = lax.dot_general(wqt_ref[nope_all:, :], c_q, _NT, preferred_element_type=F32)
    cos_t = cost_ref[...]
    sin_t = sint_ref[...]
    zero = jnp.zeros((QK_ROPE_DIM, tm), BF16)
    for head in range(MLA_HEADS):
        lo = head * MLA_QK_PAD
        nope = q_nope_t[head * QK_NOPE_DIM:(head + 1) * QK_NOPE_DIM, :]
        rope = _rope_rows(q_rope_t[head * QK_ROPE_DIM:(head + 1) * QK_ROPE_DIM, :], cos_t, sin_t)
        rope = (rope * MLA_QSCALE).astype(BF16)
        qcat_t_ref[lo:lo + QK_NOPE_DIM, :] = (nope * MLA_QSCALE).astype(BF16)
        qcat_t_ref[lo + QK_NOPE_DIM:lo + MLA_QK_PAD, :] = jnp.concatenate(
            [zero, rope] if head % 2 else [rope, zero], axis=0)


def _mla_proj(x2d, g_kv, g_q, w_kv_a, g_kv_a, w_k_b, w_vt_b, w_q_a, g_q_a, w_qt_b, rope, seq):
    m = x2d.shape[0]
    tm = ROW_TILE
    row = lambda i: (i, 0)
    col = lambda i: (0, i)
    fixed = lambda i: (0, 0)
    pos = lambda i: (i % (seq // tm), 0)
    pos_t = lambda i: (0, i % (seq // tm))
    kva_cols = w_kv_a.shape[1]
    cat = MLA_HEADS * MLA_QK_PAD
    pipelined = tm * D_MODEL * 4 + 2 * tm * V7X_LANES * 4 + tm * (2 * cat + D_MODEL) * 2
    resident = (D_MODEL * kva_cols + KV_LORA_RANK * 2 * D_MODEL + D_MODEL * Q_LORA_RANK
                + Q_LORA_RANK * 3 * D_MODEL // 2) * 2 + 4 * D_MODEL * 4
    temps = tm * D_MODEL * (4 + 2 + 2) + 4 * tm * D_MODEL * 4
    return pl.pallas_call(
        _mla_proj_kernel,
        grid=(m // tm,),
        in_specs=[
            pl.BlockSpec((tm, D_MODEL), row),
            _resident((1, D_MODEL), fixed),
            _resident((1, D_MODEL), fixed),
            _resident((D_MODEL, kva_cols), fixed),
            _resident((1, KV_LORA_RANK), fixed),
            _resident((KV_LORA_RANK, MLA_HEADS * QK_NOPE_DIM), fixed),
            _resident((MLA_HEADS * V_HEAD_DIM, KV_LORA_RANK), fixed),
            _resident((D_MODEL, Q_LORA_RANK), fixed),
            _resident((1, Q_LORA_RANK), fixed),
            _resident((3 * D_MODEL // 2, Q_LORA_RANK), fixed),
            pl.BlockSpec((tm, V7X_LANES), pos),
            pl.BlockSpec((tm, V7X_LANES), pos),
            pl.BlockSpec((QK_ROPE_DIM, tm), pos_t),
            pl.BlockSpec((QK_ROPE_DIM, tm), pos_t),
        ],
        out_specs=[
            pl.BlockSpec((cat, tm), col),
            pl.BlockSpec((tm, cat), row),
            pl.BlockSpec((MLA_HEADS * V_HEAD_DIM, tm), col),
        ],
        out_shape=[
            jax.ShapeDtypeStruct((cat, m), BF16),
            jax.ShapeDtypeStruct((m, cat), BF16),
            jax.ShapeDtypeStruct((MLA_HEADS * V_HEAD_DIM, m), BF16),
        ],
        compiler_params=_compiler_params(1, pipelined, resident, temps),
        name="mla_proj",
    )(x2d, g_kv, g_q, w_kv_a, g_kv_a, w_k_b, w_vt_b, w_q_a, g_q_a, w_qt_b, *rope)


MLA_HEADS_PER_STEP = 2


def _mla_attn_kernel(qt_ref, k_ref, vt_ref, o_ref, s_scr, bmax_scr, m_scr, acc_scr):
    n = MLA_HEADS_PER_STEP
    tk = s_scr.shape[2]

    def score_fn(j, h, strip):
        feats = slice(h * MLA_QK_PAD, (h + 1) * MLA_QK_PAD)
        k = k_ref[0, pl.ds(pl.multiple_of(j * tk, tk), tk), feats]
        return _dot(k, qt_ref[feats, strip * V7X_MXU_WIDTH:(strip + 1) * V7X_MXU_WIDTH])

    def value_fn(j, h):
        rows = slice(h * V_HEAD_DIM, (h + 1) * V_HEAD_DIM)
        return vt_ref[rows, pl.ds(pl.multiple_of(j * tk, tk), tk)]

    _causal_attention(pl.program_id(2), n, score_fn, value_fn, s_scr, bmax_scr, m_scr, acc_scr)
    o_ref[0] = _attention_output(acc_scr, n, V_HEAD_DIM).T.astype(o_ref.dtype)


def _mla_attn(q_cat_t, k_cat, v_t, b, s):
    t = ATTN_Q_TILE
    n = MLA_HEADS_PER_STEP
    pipelined = n * (t * MLA_QK_PAD * 2 + s * MLA_QK_PAD * 2 + s * V_HEAD_DIM * 2
                     + t * V_HEAD_DIM * 2)
    temps = (_attention_scratch_bytes(t, n, V_HEAD_DIM)
             + n * (t * t // 2 * (4 + 2) + t * V_HEAD_DIM * 4))
    return pl.pallas_call(
        _mla_attn_kernel,
        grid=(b, MLA_HEADS // n, s // t),
        in_specs=[
            pl.BlockSpec((n * MLA_QK_PAD, t), lambda i, h, j: (h, i * (s // t) + j)),
            pl.BlockSpec((1, s, n * MLA_QK_PAD), lambda i, h, j: (i, 0, h)),
            pl.BlockSpec((n * V_HEAD_DIM, s), lambda i, h, j: (h, i)),
        ],
        out_specs=pl.BlockSpec((1, t, n * V_HEAD_DIM), lambda i, h, j: (i, j, h)),
        out_shape=jax.ShapeDtypeStruct((b, s, MLA_HEADS * V_HEAD_DIM), BF16),
        scratch_shapes=_attention_scratch(t, n, V_HEAD_DIM),
        compiler_params=_compiler_params(3, pipelined, 0, temps, ATTN_FLAGS),
        name="mla_attn",
    )(q_cat_t, k_cat, v_t)


def _rope_tables(seq):
    half = QK_ROPE_DIM // 2
    inv = 1.0 / (ROPE_BASE ** (jnp.arange(0, QK_ROPE_DIM, 2, dtype=F32) / QK_ROPE_DIM))
    ang = jnp.arange(seq, dtype=F32)[:, None] * inv[None, :]
    cos, sin = jnp.cos(ang), jnp.sin(ang)
    cos_head = jnp.concatenate([cos, cos], axis=1)
    sin_head = jnp.concatenate([-sin, sin], axis=1)
    reps = V7X_LANES // QK_ROPE_DIM
    return (jnp.tile(cos_head, (1, reps)), jnp.tile(sin_head, (1, reps)), cos_head.T, sin_head.T)


def _row(vec):
    return vec.reshape(1, -1).astype(F32)


def kernel(x, norm_mix_g, norm_ffn_g, fox_w_in, fox_b_f, fox_w_out, kv_norm_g, mla_w_kv_a,
           mla_kv_a_norm_g, mla_w_kv_b, mla_w_q_a, mla_q_a_norm_g, mla_w_q_b, mla_w_out,
           ffn_w_up, ffn_w_down, final_norm_g):
    b, s, d = x.shape
    assert d == D_MODEL and s % ATTN_Q_TILE == 0 and s % SEQ_TILE == 0 and s % ROW_TILE == 0
    assert fox_w_in.shape[0] == 1 and mla_w_q_a.shape[0] == 1
    x2d = x.reshape(b * s, d)

    w_in = fox_w_in[0]
    w_k = w_in[:, FOX_WIDTH:2 * FOX_WIDTH].astype(BF16)
    w_qvt = jnp.concatenate([w_in[:, :FOX_WIDTH], w_in[:, 2 * FOX_WIDTH:3 * FOX_WIDTH]],
                            axis=1).T.astype(BF16)
    w_f = jnp.pad(w_in[:, 3 * FOX_WIDTH:], ((0, 0), (0, V7X_LANES - FOX_HEADS))).astype(BF16)
    b_f = jnp.pad(fox_b_f[0].astype(F32), (0, V7X_LANES - FOX_HEADS)).reshape(1, V7X_LANES)
    q_t, k, v_t, f_logit = _fox_inproj(x2d, _row(norm_mix_g[0]), w_k, w_qvt, w_f)
    ext = _gate_cumsum(f_logit.reshape(b, s, V7X_LANES), b_f)
    ctx = _fox_attn(q_t, k.reshape(b, s, FOX_WIDTH), ext, v_t)
    x2d = _mix_ffn(x2d, ctx.reshape(b * s, FOX_WIDTH), fox_w_out[0].astype(BF16),
                   _row(norm_ffn_g[0]), ffn_w_up[0].astype(BF16), ffn_w_down[0].astype(BF16),
                   _row(final_norm_g), final_norm=False)

    w_kv_a = jnp.pad(mla_w_kv_a, ((0, 0), (0, V7X_LANES - QK_ROPE_DIM))).astype(BF16)
    w_kv_b = mla_w_kv_b.reshape(KV_LORA_RANK, MLA_HEADS, QK_NOPE_DIM + V_HEAD_DIM)
    w_k_b = w_kv_b[:, :, :QK_NOPE_DIM].reshape(KV_LORA_RANK, -1).astype(BF16)
    w_vt_b = w_kv_b[:, :, QK_NOPE_DIM:].reshape(KV_LORA_RANK, -1).T.astype(BF16)
    w_q_b = mla_w_q_b[0].reshape(Q_LORA_RANK, MLA_HEADS, QK_NOPE_DIM + QK_ROPE_DIM)
    w_qt_b = jnp.concatenate(
        [w_q_b[:, :, :QK_NOPE_DIM].reshape(Q_LORA_RANK, -1),
         w_q_b[:, :, QK_NOPE_DIM:].reshape(Q_LORA_RANK, -1)], axis=1).T.astype(BF16)
    q_cat_t, k_cat, v_t = _mla_proj(
        x2d, _row(kv_norm_g), _row(norm_mix_g[1]), w_kv_a, _row(mla_kv_a_norm_g), w_k_b, w_vt_b,
        mla_w_q_a[0].astype(BF16), _row(mla_q_a_norm_g[0]), w_qt_b, _rope_tables(s), s)

    cat = MLA_HEADS * MLA_QK_PAD
    ctx = _mla_attn(q_cat_t, k_cat.reshape(b, s, cat), v_t, b, s)
    out = _mix_ffn(x2d, ctx.reshape(b * s, MLA_HEADS * V_HEAD_DIM), mla_w_out[0].astype(BF16),
                   _row(norm_ffn_g[1]), ffn_w_up[1].astype(BF16), ffn_w_down[1].astype(BF16),
                   _row(final_norm_g), final_norm=True)
    return out.reshape(b, s, d)
```

```python
import functools
import math

import jax
import jax.numpy as jnp
import numpy as np
from jax import lax
from jax.experimental import pallas as pl
from jax.experimental.pallas import tpu as pltpu

D_MODEL = 1024
FOX_HEADS = 16
FOX_HEAD_DIM = 64
FOX_WIDTH = FOX_HEADS * FOX_HEAD_DIM
MLA_HEADS = 8
QK_NOPE_DIM = 128
QK_ROPE_DIM = 64
V_HEAD_DIM = 128
Q_LORA_RANK = 384
KV_LORA_RANK = 256
ROPE_BASE = 10000.0
D_FF = 4 * D_MODEL
EPS = 1e-6

V7X_LANES = 128
V7X_VMEM_BYTES = 64 * 1024 * 1024
BF16_SUBLANES = 16
V7X_MXU_WIDTH = 256

LOG2E = 1.4426950408889634
FOX_QSCALE = (FOX_HEAD_DIM ** -0.5) * LOG2E
MLA_QSCALE = ((QK_NOPE_DIM + QK_ROPE_DIM) ** -0.5) * LOG2E
MLA_QK_PAD = 256

ROW_TILE = 512
SEQ_TILE = 512
ATTN_Q_TILE = 1024
FF_CHUNK = 512

BF16 = jnp.bfloat16
F32 = jnp.float32
_NT = (((1,), (1,)), ((), ()))


def _compiler_params(n_grid, pipelined_bytes, resident_bytes, temp_bytes, flags=None):
    need = 2 * pipelined_bytes + resident_bytes + temp_bytes
    return pltpu.CompilerParams(
        dimension_semantics=("arbitrary",) * n_grid,
        vmem_limit_bytes=min(need, V7X_VMEM_BYTES),
        flags=flags,
    )


ATTN_FLAGS = None


def _resident(shape, index_map):
    return pl.BlockSpec(shape, index_map, pipeline_mode=pl.Buffered(1))


def _rms_unit(xf):
    return xf * lax.rsqrt(jnp.mean(xf * xf, axis=-1, keepdims=True) + EPS)


def _dot(a, b):
    return jnp.dot(a, b, preferred_element_type=F32)


def _fox_inproj_kernel(x_ref, g_ref, w_ref, qt_ref, k_ref, vt_ref, f_ref,
                       wk_scr, wqvt_scr, wf_scr):
    @pl.when(pl.program_id(0) == 0)
    def _():
        chunk = V7X_MXU_WIDTH
        wk_scr[...] = w_ref[:, FOX_WIDTH:2 * FOX_WIDTH].astype(BF16)
        for dst, src in ((0, 0), (FOX_WIDTH, 2 * FOX_WIDTH)):
            for c in range(0, FOX_WIDTH, chunk):
                block = w_ref[:, src + c:src + c + chunk]
                wqvt_scr[dst + c:dst + c + chunk, :] = block.T.astype(BF16)
        wf_scr[...] = jnp.zeros_like(wf_scr)
        wf_scr[:, 0:FOX_HEADS] = w_ref[:, 3 * FOX_WIDTH:3 * FOX_WIDTH + FOX_HEADS].astype(BF16)

    h = (_rms_unit(x_ref[...]) * g_ref[...]).astype(BF16)
    k_ref[...] = _dot(h, wk_scr[...]).astype(BF16)
    q_t = lax.dot_general(wqvt_scr[0:FOX_WIDTH, :], h, _NT, preferred_element_type=F32)
    qt_ref[...] = (q_t * FOX_QSCALE).astype(BF16)
    v_t = lax.dot_general(wqvt_scr[FOX_WIDTH:2 * FOX_WIDTH, :], h, _NT, preferred_element_type=F32)
    vt_ref[...] = v_t.astype(BF16)
    f_ref[...] = _dot(h, wf_scr[...])


def _fox_inproj(x2d, g, w_in):
    m = x2d.shape[0]
    tm = ROW_TILE
    in_cols = w_in.shape[-1]
    row = lambda i: (i, 0)
    col = lambda i: (0, i)
    fixed = lambda i: (0, 0)
    feature_major = jax.ShapeDtypeStruct((FOX_WIDTH, m), BF16)
    pipelined = tm * D_MODEL * 4 + 3 * tm * FOX_WIDTH * 2 + tm * V7X_LANES * 4
    resident = D_MODEL * 4 + D_MODEL * in_cols * 4
    scratch = D_MODEL * 3 * FOX_WIDTH * 2 + D_MODEL * V7X_LANES * 2
    temps = scratch + tm * D_MODEL * (4 + 2) + 2 * tm * FOX_WIDTH * 4
    return pl.pallas_call(
        _fox_inproj_kernel,
        grid=(m // tm,),
        in_specs=[
            pl.BlockSpec((tm, D_MODEL), row),
            _resident((1, D_MODEL), fixed),
            _resident((None, D_MODEL, in_cols), lambda i: (0, 0, 0)),
        ],
        out_specs=[
            pl.BlockSpec((FOX_WIDTH, tm), col),
            pl.BlockSpec((tm, FOX_WIDTH), row),
            pl.BlockSpec((FOX_WIDTH, tm), col),
            pl.BlockSpec((tm, V7X_LANES), row),
        ],
        out_shape=[feature_major, jax.ShapeDtypeStruct((m, FOX_WIDTH), BF16), feature_major,
                   jax.ShapeDtypeStruct((m, V7X_LANES), F32)],
        scratch_shapes=[pltpu.VMEM((D_MODEL, FOX_WIDTH), BF16),
                        pltpu.VMEM((2 * FOX_WIDTH, D_MODEL), BF16),
                        pltpu.VMEM((D_MODEL, V7X_LANES), BF16)],
        compiler_params=_compiler_params(1, pipelined, resident, temps),
        name="fox_inproj",
    )(x2d, g, w_in)


def _split3(x):
    hi = x.astype(BF16)
    rem = x - hi.astype(F32)
    mid = rem.astype(BF16)
    lo = (rem - mid.astype(F32)).astype(BF16)
    return hi, mid, lo


def _gate_cumsum_kernel(f_ref, b_ref, e_ref, carry_ref, *, sb):
    @pl.when(pl.program_id(1) == 0)
    def _():
        carry_ref[...] = jnp.zeros_like(carry_ref)

    heads = FOX_HEADS
    lane = lax.broadcasted_iota(jnp.int32, (sb, V7X_LANES), 1)

    def pack3(x):
        hi, mid, lo = (piece.astype(F32) for piece in _split3(x))
        packed = jnp.where(lane < heads, hi,
                           jnp.where(lane < 2 * heads, pltpu.roll(mid, heads, 1),
                                     jnp.where(lane < 3 * heads, pltpu.roll(lo, 2 * heads, 1),
                                               0.0)))
        return packed.astype(BF16)

    z = f_ref[0] + b_ref[...]
    ls = jnp.minimum(z, 0.0) - jnp.log1p(jnp.exp(-jnp.abs(z)))
    row = lax.broadcasted_iota(jnp.int32, (sb, sb), 0)
    col = lax.broadcasted_iota(jnp.int32, (sb, sb), 1)
    tri = jnp.where(row >= col, 1.0, 0.0).astype(BF16)
    sums = _dot(tri, pack3(ls))
    c = (sums + pltpu.roll(sums, V7X_LANES - heads, 1)
         + pltpu.roll(sums, V7X_LANES - 2 * heads, 1) + carry_ref[...])
    carry_ref[...] = c[sb - 1:sb, :]
    e_ref[0] = pack3(c * LOG2E)


def _gate_cumsum(f_logit, b_f):
    b, s, _ = f_logit.shape
    sb = SEQ_TILE
    pipelined = sb * V7X_LANES * (4 + 2)
    temps = 2 * sb * sb * 4 + 12 * sb * V7X_LANES * 4
    return pl.pallas_call(
        functools.partial(_gate_cumsum_kernel, sb=sb),
        grid=(b, s // sb),
        in_specs=[
            pl.BlockSpec((1, sb, V7X_LANES), lambda i, j: (i, j, 0)),
            _resident((1, V7X_LANES), lambda i, j: (0, 0)),
        ],
        out_specs=pl.BlockSpec((1, sb, V7X_LANES), lambda i, j: (i, j, 0)),
        out_shape=jax.ShapeDtypeStruct((b, s, V7X_LANES), BF16),
        scratch_shapes=[pltpu.VMEM((1, V7X_LANES), F32)],
        compiler_params=_compiler_params(2, pipelined, V7X_LANES * 4, temps),
        name="gate_cumsum",
    )(f_logit, b_f)


def _causal_attention(qi, chains, score_fn, next_tile_score_fn, value_fn,
                      s_scr, bmax_scr, m_scr, acc_scr):
    _, _, tk, tq = s_scr.shape
    assert tq == 2 * tk
    strips = tq // V7X_MXU_WIDTH
    m_scr[...] = jnp.full(m_scr.shape, -jnp.inf, F32)
    acc_scr[...] = jnp.zeros(acc_scr.shape, F32)
    ones = jnp.ones((BF16_SUBLANES, tk), BF16)

    def lanes(n):
        return slice(n * V7X_MXU_WIDTH, (n + 1) * V7X_MXU_WIDTH)

    def store_scores(s_t, slot, c, n):
        s_scr[slot, c, :, lanes(n)] = s_t
        bmax_scr[slot, c, :, lanes(n)] = jnp.max(s_t, axis=0, keepdims=True)

    def issue(j, slot, c, n):
        store_scores(score_fn(j, c, n), slot, c, n)

    def consume(j, slot, c, n, key_offset=None):
        s_t = s_scr[slot, c, :, lanes(n)]
        if key_offset is None:
            block_max = bmax_scr[slot, c, :, lanes(n)]
        else:
            key = lax.broadcasted_iota(jnp.int32, s_t.shape, 0) + key_offset
            query = lax.broadcasted_iota(jnp.int32, s_t.shape, 1) + n * V7X_MXU_WIDTH
            s_t = jnp.where(key <= query, s_t, -jnp.inf)
            block_max = jnp.max(s_t, axis=0, keepdims=True)
        m = m_scr[c, :, lanes(n)]
        m_new = jnp.maximum(m, block_max)
        p_t = jnp.exp2(s_t - m_new).astype(BF16)
        m_scr[c, :, lanes(n)] = m_new
        v_ones = jnp.concatenate([value_fn(j, c), ones], axis=0)
        acc_scr[c, :, lanes(n)] = (jnp.exp2(m - m_new) * acc_scr[c, :, lanes(n)]
                                   + _dot(v_ones, p_t))

    units = [(c, n) for n in range(strips) for c in range(chains)]

    def consume_and_issue(j, slot):
        for c, n in units:
            consume(j, slot, c, n)
            issue(j + 1, 1 - slot, c, n)

    @pl.when(qi == 0)
    def _():
        for c, n in units:
            issue(0, 0, c, n)

    def two_blocks(i, carry):
        consume_and_issue(2 * i, 0)
        consume_and_issue(2 * i + 1, 1)
        return carry

    lax.fori_loop(0, qi, two_blocks, 0)

    half = strips // 2
    for c, n in units:
        consume(2 * qi, 0, c, n, key_offset=0 if n < half else None)
        if n >= half:
            issue(2 * qi + 1, 1, c, n)
        else:
            store_scores(next_tile_score_fn(c, n), 0, c, n)
    for c, n in units:
        if n >= half:
            consume(2 * qi + 1, 1, c, n, key_offset=tk)
            store_scores(next_tile_score_fn(c, n), 0, c, n)


def _attention_scratch(tq, chains, width):
    tk = tq // 2
    return [pltpu.VMEM((2, chains, tk, tq), F32), pltpu.VMEM((2, chains, 1, tq), F32),
            pltpu.VMEM((chains, 1, tq), F32),
            pltpu.VMEM((chains, width + BF16_SUBLANES, tq), F32)]


def _attention_scratch_bytes(tq, chains, width):
    return chains * tq * tq * 4 + chains * (3 * 8 + width + BF16_SUBLANES) * tq * 4


def _attention_output(acc_scr, chains, width):
    return jnp.concatenate(
        [acc_scr[c, 0:width, :] / acc_scr[c, width:width + 1, :] for c in range(chains)], axis=0)


def _fox_attn_kernel(qt_ref, qt_next_ref, k_ref, e_ref, vt_ref, o_ref,
                     qa_scr, s_scr, bmax_scr, m_scr, acc_scr):
    tk = s_scr.shape[2]
    pair = pl.program_id(1)

    for which, ref in enumerate((qt_ref, qt_next_ref)):
        q_t = ref[...]
        row = lax.broadcasted_iota(jnp.int32, q_t.shape, 0)
        for h in range(2):
            own = (row >= h * FOX_HEAD_DIM) & (row < (h + 1) * FOX_HEAD_DIM)
            head = 2 * pair + h
            bias_row = ((row == head) | (row == head + FOX_HEADS) | (row == head + 2 * FOX_HEADS))
            qa_scr[which, h, 0:V7X_LANES, :] = jnp.where(own, q_t, jnp.zeros_like(q_t))
            qa_scr[which, h, V7X_LANES:, :] = jnp.where(bias_row, -1.0, 0.0).astype(BF16)

    def keys(j):
        ks = pl.multiple_of(j * tk, tk)
        return jnp.concatenate([k_ref[0, pl.ds(ks, tk), :], e_ref[0, pl.ds(ks, tk), :]], axis=1)

    def strip(n):
        return slice(n * V7X_MXU_WIDTH, (n + 1) * V7X_MXU_WIDTH)

    def score_fn(j, h, n):
        return _dot(keys(j), qa_scr[0, h, :, strip(n)])

    def next_tile_score_fn(h, n):
        return _dot(keys(0), qa_scr[1, h, :, strip(n)])

    def value_fn(j, h):
        rows = slice(h * FOX_HEAD_DIM, (h + 1) * FOX_HEAD_DIM)
        return vt_ref[rows, pl.ds(pl.multiple_of(j * tk, tk), tk)]

    _causal_attention(pl.program_id(2), 2, score_fn, next_tile_score_fn,
                      value_fn, s_scr, bmax_scr, m_scr, acc_scr)
    o_ref[0] = _attention_output(acc_scr, 2, FOX_HEAD_DIM).T.astype(o_ref.dtype)


def _fox_attn(q_t, k, ext, v_t):
    b, s, _ = k.shape
    t = ATTN_Q_TILE
    n_tiles = s // t
    pairs = FOX_HEADS // 2
    pipelined = 3 * t * V7X_LANES * 2 + 3 * s * V7X_LANES * 2
    scratch = 2 * 2 * 2 * V7X_LANES * t * 2 + _attention_scratch_bytes(t, 2, FOX_HEAD_DIM)
    temps = scratch + 2 * (t * t // 2 * (4 + 2) + t * V7X_LANES * 4)
    return pl.pallas_call(
        _fox_attn_kernel,
        grid=(b, pairs, n_tiles),
        in_specs=[
            pl.BlockSpec((V7X_LANES, t), lambda i, p, j: (p, i * n_tiles + j)),
            pl.BlockSpec((V7X_LANES, t),
                         lambda i, p, j: (p, i * n_tiles + jnp.minimum(j + 1, n_tiles - 1))),
            pl.BlockSpec((1, s, V7X_LANES), lambda i, p, j: (i, 0, p)),
            pl.BlockSpec((1, s, V7X_LANES), lambda i, p, j: (i, 0, 0)),
            pl.BlockSpec((V7X_LANES, s), lambda i, p, j: (p, i)),
        ],
        out_specs=pl.BlockSpec((1, t, V7X_LANES), lambda i, p, j: (i, j, p)),
        out_shape=jax.ShapeDtypeStruct((b, s, FOX_WIDTH), BF16),
        scratch_shapes=[pltpu.VMEM((2, 2, 2 * V7X_LANES, t), BF16)]
        + _attention_scratch(t, 2, FOX_HEAD_DIM),
        compiler_params=_compiler_params(3, pipelined, 0, temps, ATTN_FLAGS),
        name="fox_attn",
    )(q_t, q_t, k, ext, v_t)


def _mix_ffn_kernel(x_ref, ctx_ref, wo_ref, g_ref, wu_ref, wd_ref, gf_ref, o_ref, *, final_norm):
    x1 = x_ref[...] + _dot(ctx_ref[...], wo_ref[...])
    hn = (_rms_unit(x1) * g_ref[...]).astype(BF16)
    acc = x1
    for c in range(D_FF // FF_CHUNK):
        cols = slice(c * FF_CHUNK, (c + 1) * FF_CHUNK)
        up = jnp.maximum(_dot(hn, wu_ref[:, cols]), 0.0)
        acc = acc + _dot((up * up).astype(BF16), wd_ref[cols, :])
    if final_norm:
        acc = _rms_unit(acc) * gf_ref[...]
    o_ref[...] = acc


def _mix_ffn(x2d, ctx2d, w_out, g_ffn, w_up_all, w_down_all, layer, g_final, final_norm):
    m = x2d.shape[0]
    tm = ROW_TILE
    row = lambda i: (i, 0)
    fixed = lambda i: (0, 0)
    of_layer = lambda i: (layer, 0, 0)
    pipelined = tm * D_MODEL * (4 + 2 + 4)
    resident = (D_MODEL * D_MODEL + 2 * D_MODEL * D_FF) * 2 + 2 * D_MODEL * 4
    temps = tm * D_MODEL * (4 + 4 + 2) + tm * FF_CHUNK * (4 + 4 + 2)
    return pl.pallas_call(
        functools.partial(_mix_ffn_kernel, final_norm=final_norm),
        grid=(m // tm,),
        in_specs=[
            pl.BlockSpec((tm, D_MODEL), row),
            pl.BlockSpec((tm, D_MODEL), row),
            _resident((D_MODEL, D_MODEL), fixed),
            _resident((1, D_MODEL), fixed),
            _resident((None, D_MODEL, D_FF), of_layer),
            _resident((None, D_FF, D_MODEL), of_layer),
            _resident((1, D_MODEL), fixed),
        ],
        out_specs=pl.BlockSpec((tm, D_MODEL), row),
        out_shape=jax.ShapeDtypeStruct((m, D_MODEL), F32),
        compiler_params=_compiler_params(1, pipelined, resident, temps),
        name="mix_ffn_final" if final_norm else "mix_ffn",
    )(x2d, ctx2d, w_out, g_ffn, w_up_all, w_down_all, g_final)


def _rope_lanes(t, cos, sin_signed):
    lane = lax.broadcasted_iota(jnp.int32, t.shape, 1)
    low = (lane % QK_ROPE_DIM) < (QK_ROPE_DIM // 2)
    partner = jnp.where(low,
                        pltpu.roll(t, V7X_LANES - QK_ROPE_DIM // 2, 1),
                        pltpu.roll(t, QK_ROPE_DIM // 2, 1))
    return t * cos + partner * sin_signed


def _rope_rows(t, cos_t, sin_signed_t):
    half = QK_ROPE_DIM // 2
    partner = jnp.concatenate([t[half:, :], t[:half, :]], axis=0)
    return t * cos_t + partner * sin_signed_t


def _mla_proj_kernel(x_ref, gkv_ref, gq_ref, wkva_ref, gkva_ref, wkb_ref, wvt_ref, wqa_ref,
                     gqa_ref, wqt_ref, cos_ref, sin_ref, cost_ref, sint_ref,
                     qcat_t_ref, kcat_ref, vt_ref):
    unit = _rms_unit(x_ref[...])
    tm = unit.shape[0]
    nope_all = MLA_HEADS * QK_NOPE_DIM

    src = (unit * gkv_ref[...]).astype(BF16)
    kv_a = _dot(src, wkva_ref[...])
    c_kv = (_rms_unit(kv_a[:, 0:KV_LORA_RANK]) * gkva_ref[...]).astype(BF16)
    k_rope = _rope_lanes(kv_a[:, KV_LORA_RANK:KV_LORA_RANK + V7X_LANES],
                         cos_ref[...], sin_ref[...])
    k_rope_even = k_rope.astype(BF16)
    k_rope_odd = pltpu.roll(k_rope, QK_ROPE_DIM, 1).astype(BF16)
    k_nope = _dot(c_kv, wkb_ref[...]).astype(BF16)
    vt_ref[...] = lax.dot_general(wvt_ref[...], c_kv, _NT, preferred_element_type=F32).astype(BF16)
    for head in range(MLA_HEADS):
        lo = head * MLA_QK_PAD
        kcat_ref[:, lo:lo + QK_NOPE_DIM] = k_nope[:, head * QK_NOPE_DIM:(head + 1) * QK_NOPE_DIM]
        kcat_ref[:, lo + QK_NOPE_DIM:lo + MLA_QK_PAD] = k_rope_odd if head % 2 else k_rope_even

    h = (unit * gq_ref[...]).astype(BF16)
    c_q = (_rms_unit(_dot(h, wqa_ref[...])) * gqa_ref[...]).astype(BF16)
    q_nope_t = lax.dot_general(wqt_ref[0:nope_all, :], c_q, _NT, preferred_element_type=F32)
    q_rope_t = lax.dot_general(wqt_ref[nope_all:, :], c_q, _NT, preferred_element_type=F32)
    cos_t = cost_ref[...]
    sin_t = sint_ref[...]
    zero = jnp.zeros((QK_ROPE_DIM, tm), BF16)
    for head in range(MLA_HEADS):
        lo = head * MLA_QK_PAD
        nope = q_nope_t[head * QK_NOPE_DIM:(head + 1) * QK_NOPE_DIM, :]
        rope = _rope_rows(q_rope_t[head * QK_ROPE_DIM:(head + 1) * QK_ROPE_DIM, :], cos_t, sin_t)
        rope = (rope * MLA_QSCALE).astype(BF16)
        qcat_t_ref[lo:lo + QK_NOPE_DIM, :] = (nope * MLA_QSCALE).astype(BF16)
        qcat_t_ref[lo + QK_NOPE_DIM:lo + MLA_QK_PAD, :] = jnp.concatenate(
            [zero, rope] if head % 2 else [rope, zero], axis=0)


def _mla_proj(x2d, g_kv, g_q, w_kv_a, g_kv_a, w_k_b, w_vt_b, w_q_a, g_q_a, w_qt_b, rope, seq):
    m = x2d.shape[0]
    tm = ROW_TILE
    row = lambda i: (i, 0)
    col = lambda i: (0, i)
    fixed = lambda i: (0, 0)
    pos = lambda i: (i % (seq // tm), 0)
    pos_t = lambda i: (0, i % (seq // tm))
    kva_cols = w_kv_a.shape[1]
    cat = MLA_HEADS * MLA_QK_PAD
    pipelined = tm * D_MODEL * 4 + 2 * tm * V7X_LANES * 4 + tm * (2 * cat + D_MODEL) * 2
    resident = (D_MODEL * kva_cols + KV_LORA_RANK * 2 * D_MODEL + D_MODEL * Q_LORA_RANK
                + Q_LORA_RANK * 3 * D_MODEL // 2) * 2 + 4 * D_MODEL * 4
    temps = tm * D_MODEL * (4 + 2 + 2) + 4 * tm * D_MODEL * 4
    return pl.pallas_call(
        _mla_proj_kernel,
        grid=(m // tm,),
        in_specs=[
            pl.BlockSpec((tm, D_MODEL), row),
            _resident((1, D_MODEL), fixed),
            _resident((1, D_MODEL), fixed),
            _resident((D_MODEL, kva_cols), fixed),
            _resident((1, KV_LORA_RANK), fixed),
            _resident((KV_LORA_RANK, MLA_HEADS * QK_NOPE_DIM), fixed),
            _resident((MLA_HEADS * V_HEAD_DIM, KV_LORA_RANK), fixed),
            _resident((D_MODEL, Q_LORA_RANK), fixed),
            _resident((1, Q_LORA_RANK), fixed),
            _resident((3 * D_MODEL // 2, Q_LORA_RANK), fixed),
            pl.BlockSpec((tm, V7X_LANES), pos),
            pl.BlockSpec((tm, V7X_LANES), pos),
            pl.BlockSpec((QK_ROPE_DIM, tm), pos_t),
            pl.BlockSpec((QK_ROPE_DIM, tm), pos_t),
        ],
        out_specs=[
            pl.BlockSpec((cat, tm), col),
            pl.BlockSpec((tm, cat), row),
            pl.BlockSpec((MLA_HEADS * V_HEAD_DIM, tm), col),
        ],
        out_shape=[
            jax.ShapeDtypeStruct((cat, m), BF16),
            jax.ShapeDtypeStruct((m, cat), BF16),
            jax.ShapeDtypeStruct((MLA_HEADS * V_HEAD_DIM, m), BF16),
        ],
        compiler_params=_compiler_params(1, pipelined, resident, temps),
        name="mla_proj",
    )(x2d, g_kv, g_q, w_kv_a, g_kv_a, w_k_b, w_vt_b, w_q_a, g_q_a, w_qt_b, *rope)


MLA_HEADS_PER_STEP = 2


def _mla_attn_kernel(qt_ref, qt_next_ref, k_ref, vt_ref, o_ref, s_scr, bmax_scr, m_scr, acc_scr):
    n = MLA_HEADS_PER_STEP
    tk = s_scr.shape[2]

    def scores(q_ref, j, h, strip):
        feats = slice(h * MLA_QK_PAD, (h + 1) * MLA_QK_PAD)
        k = k_ref[0, pl.ds(pl.multiple_of(j * tk, tk), tk), feats]
        return _dot(k, q_ref[feats, strip * V7X_MXU_WIDTH:(strip + 1) * V7X_MXU_WIDTH])

    def value_fn(j, h):
        rows = slice(h * V_HEAD_DIM, (h + 1) * V_HEAD_DIM)
        return vt_ref[rows, pl.ds(pl.multiple_of(j * tk, tk), tk)]

    _causal_attention(pl.program_id(2), n,
                      functools.partial(scores, qt_ref), functools.partial(scores, qt_next_ref, 0),
                      value_fn, s_scr, bmax_scr, m_scr, acc_scr)
    o_ref[0] = _attention_output(acc_scr, n, V_HEAD_DIM).T.astype(o_ref.dtype)


def _mla_attn(q_cat_t, k_cat, v_t, b, s):
    t = ATTN_Q_TILE
    n_tiles = s // t
    n = MLA_HEADS_PER_STEP
    pipelined = n * (2 * t * MLA_QK_PAD * 2 + s * MLA_QK_PAD * 2 + s * V_HEAD_DIM * 2
                     + t * V_HEAD_DIM * 2)
    temps = (_attention_scratch_bytes(t, n, V_HEAD_DIM)
             + n * (t * t // 2 * (4 + 2) + t * V_HEAD_DIM * 4))
    return pl.pallas_call(
        _mla_attn_kernel,
        grid=(b, MLA_HEADS // n, n_tiles),
        in_specs=[
            pl.BlockSpec((n * MLA_QK_PAD, t), lambda i, h, j: (h, i * n_tiles + j)),
            pl.BlockSpec((n * MLA_QK_PAD, t),
                         lambda i, h, j: (h, i * n_tiles + jnp.minimum(j + 1, n_tiles - 1))),
            pl.BlockSpec((1, s, n * MLA_QK_PAD), lambda i, h, j: (i, 0, h)),
            pl.BlockSpec((n * V_HEAD_DIM, s), lambda i, h, j: (h, i)),
        ],
        out_specs=pl.BlockSpec((1, t, n * V_HEAD_DIM), lambda i, h, j: (i, j, h)),
        out_shape=jax.ShapeDtypeStruct((b, s, MLA_HEADS * V_HEAD_DIM), BF16),
        scratch_shapes=_attention_scratch(t, n, V_HEAD_DIM),
        compiler_params=_compiler_params(3, pipelined, 0, temps, ATTN_FLAGS),
        name="mla_attn",
    )(q_cat_t, q_cat_t, k_cat, v_t)


def _rope_tables(seq):
    f32 = np.float32
    inv = (f32(1.0) / (f32(ROPE_BASE) ** (np.arange(0, QK_ROPE_DIM, 2, dtype=f32)
                                          / f32(QK_ROPE_DIM)))).astype(f32)
    ang = np.arange(seq, dtype=f32)[:, None] * inv[None, :]
    cos, sin = np.cos(ang), np.sin(ang)
    cos_head = np.concatenate([cos, cos], axis=1)
    sin_head = np.concatenate([-sin, sin], axis=1)
    reps = V7X_LANES // QK_ROPE_DIM
    tables = (np.tile(cos_head, (1, reps)), np.tile(sin_head, (1, reps)), cos_head.T, sin_head.T)
    return tuple(jnp.asarray(np.ascontiguousarray(t), dtype=F32) for t in tables)


def _row(vec):
    return vec.reshape(1, -1).astype(F32)


def kernel(x, norm_mix_g, norm_ffn_g, fox_w_in, fox_b_f, fox_w_out, kv_norm_g, mla_w_kv_a,
           mla_kv_a_norm_g, mla_w_kv_b, mla_w_q_a, mla_q_a_norm_g, mla_w_q_b, mla_w_out,
           ffn_w_up, ffn_w_down, final_norm_g):
    b, s, d = x.shape
    assert d == D_MODEL and s % ATTN_Q_TILE == 0 and s % SEQ_TILE == 0 and s % ROW_TILE == 0
    assert fox_w_in.shape[0] == 1 and mla_w_q_a.shape[0] == 1
    x2d = x.reshape(b * s, d)

    b_f = jnp.pad(fox_b_f[0].astype(F32), (0, V7X_LANES - FOX_HEADS)).reshape(1, V7X_LANES)
    q_t, k, v_t, f_logit = _fox_inproj(x2d, _row(norm_mix_g[0]), fox_w_in)
    ext = _gate_cumsum(f_logit.reshape(b, s, V7X_LANES), b_f)
    ctx = _fox_attn(q_t, k.reshape(b, s, FOX_WIDTH), ext, v_t)
    w_up_all = ffn_w_up.astype(BF16)
    w_down_all = ffn_w_down.astype(BF16)
    x2d = _mix_ffn(x2d, ctx.reshape(b * s, FOX_WIDTH), fox_w_out[0].astype(BF16),
                   _row(norm_ffn_g[0]), w_up_all, w_down_all, 0, _row(final_norm_g),
                   final_norm=False)

    w_kv_a = jnp.pad(mla_w_kv_a, ((0, 0), (0, V7X_LANES - QK_ROPE_DIM))).astype(BF16)
    w_kv_b = mla_w_kv_b.reshape(KV_LORA_RANK, MLA_HEADS, QK_NOPE_DIM + V_HEAD_DIM)
    w_k_b = w_kv_b[:, :, :QK_NOPE_DIM].reshape(KV_LORA_RANK, -1).astype(BF16)
    w_vt_b = w_kv_b[:, :, QK_NOPE_DIM:].reshape(KV_LORA_RANK, -1).T.astype(BF16)
    w_q_b = mla_w_q_b[0].reshape(Q_LORA_RANK, MLA_HEADS, QK_NOPE_DIM + QK_ROPE_DIM)
    w_qt_b = jnp.concatenate(
        [w_q_b[:, :, :QK_NOPE_DIM].reshape(Q_LORA_RANK, -1),
         w_q_b[:, :, QK_NOPE_DIM:].reshape(Q_LORA_RANK, -1)], axis=1).T.astype(BF16)
    q_cat_t, k_cat, v_t = _mla_proj(
        x2d, _row(kv_norm_g), _row(norm_mix_g[1]), w_kv_a, _row(mla_kv_a_norm_g), w_k_b, w_vt_b,
        mla_w_q_a[0].astype(BF16), _row(mla_q_a_norm_g[0]), w_qt_b, _rope_tables(s), s)

    cat = MLA_HEADS * MLA_QK_PAD
    ctx = _mla_attn(q_cat_t, k_cat.reshape(b, s, cat), v_t, b, s)
    out = _mix_ffn(x2d, ctx.reshape(b * s, MLA_HEADS * V_HEAD_DIM), mla_w_out[0].astype(BF16),
                   _row(norm_ffn_g[1]), w_up_all, w_down_all, 1, _row(final_norm_g),
                   final_norm=True)
    return out.reshape(b, s, d)
```

```python
import functools
import math

import jax
import jax.numpy as jnp
import numpy as np
from jax import lax
from jax.experimental import pallas as pl
from jax.experimental.pallas import tpu as pltpu

D_MODEL = 1024
FOX_HEADS = 16
FOX_HEAD_DIM = 64
FOX_WIDTH = FOX_HEADS * FOX_HEAD_DIM
MLA_HEADS = 8
QK_NOPE_DIM = 128
QK_ROPE_DIM = 64
V_HEAD_DIM = 128
Q_LORA_RANK = 384
KV_LORA_RANK = 256
ROPE_BASE = 10000.0
D_FF = 4 * D_MODEL
EPS = 1e-6

V7X_LANES = 128
V7X_VMEM_BYTES = 64 * 1024 * 1024
BF16_SUBLANES = 16
V7X_MXU_WIDTH = 256

LOG2E = 1.4426950408889634
FOX_QSCALE = (FOX_HEAD_DIM ** -0.5) * LOG2E
MLA_QSCALE = ((QK_NOPE_DIM + QK_ROPE_DIM) ** -0.5) * LOG2E
MLA_QK_PAD = 256

ROW_TILE = 512
SEQ_TILE = 512
ATTN_Q_TILE = 1024
FF_CHUNK = 512

BF16 = jnp.bfloat16
F32 = jnp.float32
_NT = (((1,), (1,)), ((), ()))


def _compiler_params(n_grid, pipelined_bytes, resident_bytes, temp_bytes, flags=None):
    need = 2 * pipelined_bytes + resident_bytes + temp_bytes
    return pltpu.CompilerParams(
        dimension_semantics=("arbitrary",) * n_grid,
        vmem_limit_bytes=min(need, V7X_VMEM_BYTES),
        flags=flags,
    )


ATTN_FLAGS = None


def _resident(shape, index_map):
    return pl.BlockSpec(shape, index_map, pipeline_mode=pl.Buffered(1))


def _rms_unit(xf):
    return xf * lax.rsqrt(jnp.mean(xf * xf, axis=-1, keepdims=True) + EPS)


def _dot(a, b):
    return jnp.dot(a, b, preferred_element_type=F32)


def _fox_inproj_kernel(x_ref, g_ref, w_ref, qt_ref, k_ref, vt_ref, f_ref,
                       wk_scr, wqvt_scr, wf_scr):
    @pl.when(pl.program_id(0) == 0)
    def _():
        chunk = V7X_MXU_WIDTH
        wk_scr[...] = w_ref[:, FOX_WIDTH:2 * FOX_WIDTH].astype(BF16)
        for dst, src in ((0, 0), (FOX_WIDTH, 2 * FOX_WIDTH)):
            for c in range(0, FOX_WIDTH, chunk):
                block = w_ref[:, src + c:src + c + chunk]
                wqvt_scr[dst + c:dst + c + chunk, :] = block.T.astype(BF16)
        wf_scr[...] = jnp.zeros_like(wf_scr)
        wf_scr[:, 0:FOX_HEADS] = w_ref[:, 3 * FOX_WIDTH:3 * FOX_WIDTH + FOX_HEADS].astype(BF16)

    h = (_rms_unit(x_ref[...]) * g_ref[...]).astype(BF16)
    k_ref[...] = _dot(h, wk_scr[...]).astype(BF16)
    q_t = lax.dot_general(wqvt_scr[0:FOX_WIDTH, :], h, _NT, preferred_element_type=F32)
    qt_ref[...] = (q_t * FOX_QSCALE).astype(BF16)
    v_t = lax.dot_general(wqvt_scr[FOX_WIDTH:2 * FOX_WIDTH, :], h, _NT, preferred_element_type=F32)
    vt_ref[...] = v_t.astype(BF16)
    f_ref[...] = _dot(h, wf_scr[...])


def _fox_inproj(x2d, g, w_in):
    m = x2d.shape[0]
    tm = ROW_TILE
    in_cols = w_in.shape[-1]
    row = lambda i: (i, 0)
    col = lambda i: (0, i)
    fixed = lambda i: (0, 0)
    feature_major = jax.ShapeDtypeStruct((FOX_WIDTH, m), BF16)
    pipelined = tm * D_MODEL * 4 + 3 * tm * FOX_WIDTH * 2 + tm * V7X_LANES * 4
    resident = D_MODEL * 4 + D_MODEL * in_cols * 4
    scratch = D_MODEL * 3 * FOX_WIDTH * 2 + D_MODEL * V7X_LANES * 2
    temps = scratch + tm * D_MODEL * (4 + 2) + 2 * tm * FOX_WIDTH * 4
    return pl.pallas_call(
        _fox_inproj_kernel,
        grid=(m // tm,),
        in_specs=[
            pl.BlockSpec((tm, D_MODEL), row),
            _resident((1, D_MODEL), fixed),
            _resident((D_MODEL, in_cols), fixed),
        ],
        out_specs=[
            pl.BlockSpec((FOX_WIDTH, tm), col),
            pl.BlockSpec((tm, FOX_WIDTH), row),
            pl.BlockSpec((FOX_WIDTH, tm), col),
            pl.BlockSpec((tm, V7X_LANES), row),
        ],
        out_shape=[feature_major, jax.ShapeDtypeStruct((m, FOX_WIDTH), BF16), feature_major,
                   jax.ShapeDtypeStruct((m, V7X_LANES), F32)],
        scratch_shapes=[pltpu.VMEM((D_MODEL, FOX_WIDTH), BF16),
                        pltpu.VMEM((2 * FOX_WIDTH, D_MODEL), BF16),
                        pltpu.VMEM((D_MODEL, V7X_LANES), BF16)],
        compiler_params=_compiler_params(1, pipelined, resident, temps),
        name="fox_inproj",
    )(x2d, g, w_in)


def _split3(x):
    hi = x.astype(BF16)
    rem = x - hi.astype(F32)
    mid = rem.astype(BF16)
    lo = (rem - mid.astype(F32)).astype(BF16)
    return hi, mid, lo


def _gate_cumsum_kernel(f_ref, b_ref, e_ref, carry_ref, *, sb):
    @pl.when(pl.program_id(1) == 0)
    def _():
        carry_ref[...] = jnp.zeros_like(carry_ref)

    heads = FOX_HEADS
    lane = lax.broadcasted_iota(jnp.int32, (sb, V7X_LANES), 1)

    def pack3(x):
        hi, mid, lo = (piece.astype(F32) for piece in _split3(x))
        packed = jnp.where(lane < heads, hi,
                           jnp.where(lane < 2 * heads, pltpu.roll(mid, heads, 1),
                                     jnp.where(lane < 3 * heads, pltpu.roll(lo, 2 * heads, 1),
                                               0.0)))
        return packed.astype(BF16)

    z = f_ref[0] + b_ref[...]
    ls = jnp.minimum(z, 0.0) - jnp.log1p(jnp.exp(-jnp.abs(z)))
    row = lax.broadcasted_iota(jnp.int32, (sb, sb), 0)
    col = lax.broadcasted_iota(jnp.int32, (sb, sb), 1)
    tri = jnp.where(row >= col, 1.0, 0.0).astype(BF16)
    sums = _dot(tri, pack3(ls))
    c = (sums + pltpu.roll(sums, V7X_LANES - heads, 1)
         + pltpu.roll(sums, V7X_LANES - 2 * heads, 1) + carry_ref[...])
    carry_ref[...] = c[sb - 1:sb, :]
    e_ref[0] = pack3(c * LOG2E)


def _gate_cumsum(f_logit, b_f):
    b, s, _ = f_logit.shape
    sb = SEQ_TILE
    pipelined = sb * V7X_LANES * (4 + 2)
    temps = 2 * sb * sb * 4 + 12 * sb * V7X_LANES * 4
    return pl.pallas_call(
        functools.partial(_gate_cumsum_kernel, sb=sb),
        grid=(b, s // sb),
        in_specs=[
            pl.BlockSpec((1, sb, V7X_LANES), lambda i, j: (i, j, 0)),
            _resident((1, V7X_LANES), lambda i, j: (0, 0)),
        ],
        out_specs=pl.BlockSpec((1, sb, V7X_LANES), lambda i, j: (i, j, 0)),
        out_shape=jax.ShapeDtypeStruct((b, s, V7X_LANES), BF16),
        scratch_shapes=[pltpu.VMEM((1, V7X_LANES), F32)],
        compiler_params=_compiler_params(2, pipelined, V7X_LANES * 4, temps),
        name="gate_cumsum",
    )(f_logit, b_f)


def _causal_attention(qi, chains, score_fn, next_tile_score_fn, value_fn,
                      s_scr, bmax_scr, m_scr, acc_scr):
    _, _, tk, tq = s_scr.shape
    assert tq == 2 * tk
    strips = tq // V7X_MXU_WIDTH
    m_scr[...] = jnp.full(m_scr.shape, -jnp.inf, F32)
    acc_scr[...] = jnp.zeros(acc_scr.shape, F32)
    ones = jnp.ones((BF16_SUBLANES, tk), BF16)

    def lanes(n):
        return slice(n * V7X_MXU_WIDTH, (n + 1) * V7X_MXU_WIDTH)

    def store_scores(s_t, slot, c, n):
        s_scr[slot, c, :, lanes(n)] = s_t
        bmax_scr[slot, c, :, lanes(n)] = jnp.max(s_t, axis=0, keepdims=True)

    def issue(j, slot, c, n):
        store_scores(score_fn(j, c, n, tk), slot, c, n)

    def visible_keys(n, key_offset):
        return min(tk, (n + 1) * V7X_MXU_WIDTH - key_offset)

    def issue_masked(j, slot, c, n, key_offset):
        nk = visible_keys(n, key_offset)
        s_scr[slot, c, 0:nk, lanes(n)] = score_fn(j, c, n, nk)

    def consume(j, slot, c, n, key_offset=None):
        nk = tk if key_offset is None else visible_keys(n, key_offset)
        s_t = s_scr[slot, c, 0:nk, lanes(n)]
        if key_offset is None:
            block_max = bmax_scr[slot, c, :, lanes(n)]
        else:
            key = lax.broadcasted_iota(jnp.int32, s_t.shape, 0) + key_offset
            query = lax.broadcasted_iota(jnp.int32, s_t.shape, 1) + n * V7X_MXU_WIDTH
            s_t = jnp.where(key <= query, s_t, -jnp.inf)
            block_max = jnp.max(s_t, axis=0, keepdims=True)
        m = m_scr[c, :, lanes(n)]
        m_new = jnp.maximum(m, block_max)
        p_t = jnp.exp2(s_t - m_new).astype(BF16)
        m_scr[c, :, lanes(n)] = m_new
        v_ones = jnp.concatenate([value_fn(j, c, nk), ones[:, 0:nk]], axis=0)
        acc_scr[c, :, lanes(n)] = (jnp.exp2(m - m_new) * acc_scr[c, :, lanes(n)]
                                   + _dot(v_ones, p_t))

    units = [(c, n) for n in range(strips) for c in range(chains)]

    def consume_and_issue(j, slot):
        for c, n in units:
            consume(j, slot, c, n)
            issue(j + 1, 1 - slot, c, n)

    @pl.when(qi == 0)
    def _():
        for c, n in units:
            issue(0, 0, c, n)

    def two_blocks(i):
        consume_and_issue(2 * i, 0)
        consume_and_issue(2 * i + 1, 1)

    def four_blocks(i, carry):
        two_blocks(2 * i)
        two_blocks(2 * i + 1)
        return carry

    lax.fori_loop(0, qi // 2, four_blocks, 0)

    @pl.when(qi % 2 == 1)
    def _():
        two_blocks(qi - 1)

    half = strips // 2
    for c, n in units:
        consume(2 * qi, 0, c, n, key_offset=0 if n < half else None)
        if n >= half:
            issue_masked(2 * qi + 1, 1, c, n, tk)
        else:
            store_scores(next_tile_score_fn(c, n), 0, c, n)
    for c, n in units:
        if n >= half:
            consume(2 * qi + 1, 1, c, n, key_offset=tk)
            store_scores(next_tile_score_fn(c, n), 0, c, n)


def _attention_scratch(tq, chains, width):
    tk = tq // 2
    return [pltpu.VMEM((2, chains, tk, tq), F32), pltpu.VMEM((2, chains, 1, tq), F32),
            pltpu.VMEM((chains, 1, tq), F32),
            pltpu.VMEM((chains, width + BF16_SUBLANES, tq), F32)]


def _attention_scratch_bytes(tq, chains, width):
    return chains * tq * tq * 4 + chains * (3 * 8 + width + BF16_SUBLANES) * tq * 4


def _attention_output(acc_scr, chains, width):
    return jnp.concatenate(
        [acc_scr[c, 0:width, :] / acc_scr[c, width:width + 1, :] for c in range(chains)], axis=0)


def _fox_attn_kernel(qt_ref, qt_next_ref, k_ref, e_ref, vt_ref, o_ref,
                     qa_scr, s_scr, bmax_scr, m_scr, acc_scr):
    tk = s_scr.shape[2]
    pair = pl.program_id(1)
    qi = pl.program_id(2)
    cur = qi % 2

    def put_features(slot, ref):
        q_t = ref[...]
        row = lax.broadcasted_iota(jnp.int32, q_t.shape, 0)
        for h in range(2):
            own = (row >= h * FOX_HEAD_DIM) & (row < (h + 1) * FOX_HEAD_DIM)
            qa_scr[slot, h, 0:V7X_LANES, :] = jnp.where(own, q_t, jnp.zeros_like(q_t))

    @pl.when(qi == 0)
    def _():
        put_features(0, qt_ref)
        row = lax.broadcasted_iota(jnp.int32, qt_ref.shape, 0)
        for h in range(2):
            head = 2 * pair + h
            bias_row = ((row == head) | (row == head + FOX_HEADS) | (row == head + 2 * FOX_HEADS))
            selector = jnp.where(bias_row, -1.0, 0.0).astype(BF16)
            qa_scr[0, h, V7X_LANES:, :] = selector
            qa_scr[1, h, V7X_LANES:, :] = selector

    put_features(1 - cur, qt_next_ref)

    def keys(j, nk):
        ks = pl.multiple_of(j * tk, tk)
        return jnp.concatenate([k_ref[0, pl.ds(ks, nk), :], e_ref[0, pl.ds(ks, nk), :]], axis=1)

    def strip(n):
        return slice(n * V7X_MXU_WIDTH, (n + 1) * V7X_MXU_WIDTH)

    def score_fn(j, h, n, nk):
        return _dot(keys(j, nk), qa_scr[cur, h, :, strip(n)])

    def next_tile_score_fn(h, n):
        return _dot(keys(0, tk), qa_scr[1 - cur, h, :, strip(n)])

    def value_fn(j, h, nk):
        rows = slice(h * FOX_HEAD_DIM, (h + 1) * FOX_HEAD_DIM)
        return vt_ref[rows, pl.ds(pl.multiple_of(j * tk, tk), nk)]

    _causal_attention(pl.program_id(2), 2, score_fn, next_tile_score_fn,
                      value_fn, s_scr, bmax_scr, m_scr, acc_scr)
    o_ref[0] = _attention_output(acc_scr, 2, FOX_HEAD_DIM).T.astype(o_ref.dtype)


def _fox_attn(q_t, k, ext, v_t):
    b, s, _ = k.shape
    t = ATTN_Q_TILE
    n_tiles = s // t
    pairs = FOX_HEADS // 2
    pipelined = 3 * t * V7X_LANES * 2 + 3 * s * V7X_LANES * 2
    scratch = 2 * 2 * 2 * V7X_LANES * t * 2 + _attention_scratch_bytes(t, 2, FOX_HEAD_DIM)
    temps = scratch + 2 * (t * t // 2 * (4 + 2) + t * V7X_LANES * 4)
    return pl.pallas_call(
        _fox_attn_kernel,
        grid=(b, pairs, n_tiles),
        in_specs=[
            pl.BlockSpec((V7X_LANES, t), lambda i, p, j: (p, i * n_tiles + j)),
            pl.BlockSpec((V7X_LANES, t),
                         lambda i, p, j: (p, i * n_tiles + jnp.minimum(j + 1, n_tiles - 1))),
            pl.BlockSpec((1, s, V7X_LANES), lambda i, p, j: (i, 0, p)),
            pl.BlockSpec((1, s, V7X_LANES), lambda i, p, j: (i, 0, 0)),
            pl.BlockSpec((V7X_LANES, s), lambda i, p, j: (p, i)),
        ],
        out_specs=pl.BlockSpec((1, t, V7X_LANES), lambda i, p, j: (i, j, p)),
        out_shape=jax.ShapeDtypeStruct((b, s, FOX_WIDTH), BF16),
        scratch_shapes=[pltpu.VMEM((2, 2, 2 * V7X_LANES, t), BF16)]
        + _attention_scratch(t, 2, FOX_HEAD_DIM),
        compiler_params=_compiler_params(3, pipelined, 0, temps, ATTN_FLAGS),
        name="fox_attn",
    )(q_t, q_t, k, ext, v_t)


def _mix_ffn_kernel(x_ref, ctx_ref, wo_ref, g_ref, wu_ref, wd_ref, gf_ref, o_ref, *, final_norm):
    x1 = x_ref[...] + _dot(ctx_ref[...], wo_ref[...])
    hn = (_rms_unit(x1) * g_ref[...]).astype(BF16)
    acc = x1
    for c in range(D_FF // FF_CHUNK):
        cols = slice(c * FF_CHUNK, (c + 1) * FF_CHUNK)
        up = jnp.maximum(_dot(hn, wu_ref[:, cols]), 0.0)
        acc = acc + _dot((up * up).astype(BF16), wd_ref[cols, :])
    if final_norm:
        acc = _rms_unit(acc) * gf_ref[...]
    o_ref[...] = acc


def _mix_ffn(x2d, ctx2d, w_out, g_ffn, w_up_all, w_down_all, layer, g_final, final_norm):
    m = x2d.shape[0]
    tm = ROW_TILE
    row = lambda i: (i, 0)
    fixed = lambda i: (0, 0)
    of_layer = lambda i: (layer, 0, 0)
    pipelined = tm * D_MODEL * (4 + 2 + 4)
    resident = (D_MODEL * D_MODEL + 2 * D_MODEL * D_FF) * 2 + 2 * D_MODEL * 4
    temps = tm * D_MODEL * (4 + 4 + 2) + tm * FF_CHUNK * (4 + 4 + 2)
    return pl.pallas_call(
        functools.partial(_mix_ffn_kernel, final_norm=final_norm),
        grid=(m // tm,),
        in_specs=[
            pl.BlockSpec((tm, D_MODEL), row),
            pl.BlockSpec((tm, D_MODEL), row),
            _resident((D_MODEL, D_MODEL), fixed),
            _resident((1, D_MODEL), fixed),
            _resident((None, D_MODEL, D_FF), of_layer),
            _resident((None, D_FF, D_MODEL), of_layer),
            _resident((1, D_MODEL), fixed),
        ],
        out_specs=pl.BlockSpec((tm, D_MODEL), row),
        out_shape=jax.ShapeDtypeStruct((m, D_MODEL), F32),
        compiler_params=_compiler_params(1, pipelined, resident, temps),
        name="mix_ffn_final" if final_norm else "mix_ffn",
    )(x2d, ctx2d, w_out, g_ffn, w_up_all, w_down_all, g_final)


def _rope_lanes(t, cos, sin_signed):
    lane = lax.broadcasted_iota(jnp.int32, t.shape, 1)
    low = (lane % QK_ROPE_DIM) < (QK_ROPE_DIM // 2)
    partner = jnp.where(low,
                        pltpu.roll(t, V7X_LANES - QK_ROPE_DIM // 2, 1),
                        pltpu.roll(t, QK_ROPE_DIM // 2, 1))
    return t * cos + partner * sin_signed


def _rope_rows(t, cos_t, sin_signed_t):
    half = QK_ROPE_DIM // 2
    partner = jnp.concatenate([t[half:, :], t[:half, :]], axis=0)
    return t * cos_t + partner * sin_signed_t


def _mla_proj_kernel(x_ref, gkv_ref, gq_ref, wkva_ref, gkva_ref, wkb_ref, wvt_ref, wqa_ref,
                     gqa_ref, wqt_ref, cos_ref, sin_ref, cost_ref, sint_ref,
                     qcat_t_ref, kcat_ref, vt_ref):
    unit = _rms_unit(x_ref[...])
    tm = unit.shape[0]
    nope_all = MLA_HEADS * QK_NOPE_DIM

    src = (unit * gkv_ref[...]).astype(BF16)
    kv_a = _dot(src, wkva_ref[...])
    c_kv = (_rms_unit(kv_a[:, 0:KV_LORA_RANK]) * gkva_ref[...]).astype(BF16)
    k_rope = _rope_lanes(kv_a[:, KV_LORA_RANK:KV_LORA_RANK + V7X_LANES],
                         cos_ref[...], sin_ref[...])
    k_rope_even = k_rope.astype(BF16)
    k_rope_odd = pltpu.roll(k_rope, QK_ROPE_DIM, 1).astype(BF16)
    k_nope = _dot(c_kv, wkb_ref[...]).astype(BF16)
    vt_ref[...] = lax.dot_general(wvt_ref[...], c_kv, _NT, preferred_element_type=F32).astype(BF16)
    for head in range(MLA_HEADS):
        lo = head * MLA_QK_PAD
        kcat_ref[:, lo:lo + QK_NOPE_DIM] = k_nope[:, head * QK_NOPE_DIM:(head + 1) * QK_NOPE_DIM]
        kcat_ref[:, lo + QK_NOPE_DIM:lo + MLA_QK_PAD] = k_rope_odd if head % 2 else k_rope_even

    h = (unit * gq_ref[...]).astype(BF16)
    c_q = (_rms_unit(_dot(h, wqa_ref[...])) * gqa_ref[...]).astype(BF16)
    q_nope_t = lax.dot_general(wqt_ref[0:nope_all, :], c_q, _NT, preferred_element_type=F32)
    q_rope_t = lax.dot_general(wqt_ref[nope_all:, :], c_q, _NT, preferred_element_type=F32)
    cos_t = cost_ref[...]
    sin_t = sint_ref[...]
    zero = jnp.zeros((QK_ROPE_DIM, tm), BF16)
    for head in range(MLA_HEADS):
        lo = head * MLA_QK_PAD
        nope = q_nope_t[head * QK_NOPE_DIM:(head + 1) * QK_NOPE_DIM, :]
        rope = _rope_rows(q_rope_t[head * QK_ROPE_DIM:(head + 1) * QK_ROPE_DIM, :], cos_t, sin_t)
        rope = (rope * MLA_QSCALE).astype(BF16)
        qcat_t_ref[lo:lo + QK_NOPE_DIM, :] = (nope * MLA_QSCALE).astype(BF16)
        qcat_t_ref[lo + QK_NOPE_DIM:lo + MLA_QK_PAD, :] = jnp.concatenate(
            [zero, rope] if head % 2 else [rope, zero], axis=0)


def _mla_proj(x2d, g_kv, g_q, w_kv_a, g_kv_a, w_k_b, w_vt_b, w_q_a, g_q_a, w_qt_b, rope, seq):
    m = x2d.shape[0]
    tm = ROW_TILE
    row = lambda i: (i, 0)
    col = lambda i: (0, i)
    fixed = lambda i: (0, 0)
    pos = lambda i: (i % (seq // tm), 0)
    pos_t = lambda i: (0, i % (seq // tm))
    kva_cols = w_kv_a.shape[1]
    cat = MLA_HEADS * MLA_QK_PAD
    pipelined = tm * D_MODEL * 4 + 2 * tm * V7X_LANES * 4 + tm * (2 * cat + D_MODEL) * 2
    resident = (D_MODEL * kva_cols + KV_LORA_RANK * 2 * D_MODEL + D_MODEL * Q_LORA_RANK
                + Q_LORA_RANK * 3 * D_MODEL // 2) * 2 + 4 * D_MODEL * 4
    temps = tm * D_MODEL * (4 + 2 + 2) + 4 * tm * D_MODEL * 4
    return pl.pallas_call(
        _mla_proj_kernel,
        grid=(m // tm,),
        in_specs=[
            pl.BlockSpec((tm, D_MODEL), row),
            _resident((1, D_MODEL), fixed),
            _resident((1, D_MODEL), fixed),
            _resident((D_MODEL, kva_cols), fixed),
            _resident((1, KV_LORA_RANK), fixed),
            _resident((KV_LORA_RANK, MLA_HEADS * QK_NOPE_DIM), fixed),
            _resident((MLA_HEADS * V_HEAD_DIM, KV_LORA_RANK), fixed),
            _resident((D_MODEL, Q_LORA_RANK), fixed),
            _resident((1, Q_LORA_RANK), fixed),
            _resident((3 * D_MODEL // 2, Q_LORA_RANK), fixed),
            pl.BlockSpec((tm, V7X_LANES), pos),
            pl.BlockSpec((tm, V7X_LANES), pos),
            pl.BlockSpec((QK_ROPE_DIM, tm), pos_t),
            pl.BlockSpec((QK_ROPE_DIM, tm), pos_t),
        ],
        out_specs=[
            pl.BlockSpec((cat, tm), col),
            pl.BlockSpec((tm, cat), row),
            pl.BlockSpec((MLA_HEADS * V_HEAD_DIM, tm), col),
        ],
        out_shape=[
            jax.ShapeDtypeStruct((cat, m), BF16),
            jax.ShapeDtypeStruct((m, cat), BF16),
            jax.ShapeDtypeStruct((MLA_HEADS * V_HEAD_DIM, m), BF16),
        ],
        compiler_params=_compiler_params(1, pipelined, resident, temps),
        name="mla_proj",
    )(x2d, g_kv, g_q, w_kv_a, g_kv_a, w_k_b, w_vt_b, w_q_a, g_q_a, w_qt_b, *rope)


MLA_HEADS_PER_STEP = 2


def _mla_attn_kernel(qt_ref, qt_next_ref, k_ref, vt_ref, o_ref, s_scr, bmax_scr, m_scr, acc_scr):
    n = MLA_HEADS_PER_STEP
    tk = s_scr.shape[2]

    def scores(q_ref, j, h, strip, nk):
        feats = slice(h * MLA_QK_PAD, (h + 1) * MLA_QK_PAD)
        k = k_ref[0, pl.ds(pl.multiple_of(j * tk, tk), nk), feats]
        return _dot(k, q_ref[feats, strip * V7X_MXU_WIDTH:(strip + 1) * V7X_MXU_WIDTH])

    def value_fn(j, h, nk):
        rows = slice(h * V_HEAD_DIM, (h + 1) * V_HEAD_DIM)
        return vt_ref[rows, pl.ds(pl.multiple_of(j * tk, tk), nk)]

    _causal_attention(pl.program_id(2), n,
                      functools.partial(scores, qt_ref),
                      lambda h, strip: scores(qt_next_ref, 0, h, strip, tk),
                      value_fn, s_scr, bmax_scr, m_scr, acc_scr)
    o_ref[0] = _attention_output(acc_scr, n, V_HEAD_DIM).T.astype(o_ref.dtype)


def _mla_attn(q_cat_t, k_cat, v_t, b, s):
    t = ATTN_Q_TILE
    n_tiles = s // t
    n = MLA_HEADS_PER_STEP
    pipelined = n * (2 * t * MLA_QK_PAD * 2 + s * MLA_QK_PAD * 2 + s * V_HEAD_DIM * 2
                     + t * V_HEAD_DIM * 2)
    temps = (_attention_scratch_bytes(t, n, V_HEAD_DIM)
             + n * (t * t // 2 * (4 + 2) + t * V_HEAD_DIM * 4))
    return pl.pallas_call(
        _mla_attn_kernel,
        grid=(b, MLA_HEADS // n, n_tiles),
        in_specs=[
            pl.BlockSpec((n * MLA_QK_PAD, t), lambda i, h, j: (h, i * n_tiles + j)),
            pl.BlockSpec((n * MLA_QK_PAD, t),
                         lambda i, h, j: (h, i * n_tiles + jnp.minimum(j + 1, n_tiles - 1))),
            pl.BlockSpec((1, s, n * MLA_QK_PAD), lambda i, h, j: (i, 0, h)),
            pl.BlockSpec((n * V_HEAD_DIM, s), lambda i, h, j: (h, i)),
        ],
        out_specs=pl.BlockSpec((1, t, n * V_HEAD_DIM), lambda i, h, j: (i, j, h)),
        out_shape=jax.ShapeDtypeStruct((b, s, MLA_HEADS * V_HEAD_DIM), BF16),
        scratch_shapes=_attention_scratch(t, n, V_HEAD_DIM),
        compiler_params=_compiler_params(3, pipelined, 0, temps, ATTN_FLAGS),
        name="mla_attn",
    )(q_cat_t, q_cat_t, k_cat, v_t)


def _rope_tables(seq):
    f32 = np.float32
    inv = (f32(1.0) / (f32(ROPE_BASE) ** (np.arange(0, QK_ROPE_DIM, 2, dtype=f32)
                                          / f32(QK_ROPE_DIM)))).astype(f32)
    ang = np.arange(seq, dtype=f32)[:, None] * inv[None, :]
    cos, sin = np.cos(ang), np.sin(ang)
    cos_head = np.concatenate([cos, cos], axis=1)
    sin_head = np.concatenate([-sin, sin], axis=1)
    reps = V7X_LANES // QK_ROPE_DIM
    tables = (np.tile(cos_head, (1, reps)), np.tile(sin_head, (1, reps)), cos_head.T, sin_head.T)
    return tuple(jnp.asarray(np.ascontiguousarray(t), dtype=F32) for t in tables)


def _row(vec):
    return vec.reshape(1, -1).astype(F32)


def kernel(x, norm_mix_g, norm_ffn_g, fox_w_in, fox_b_f, fox_w_out, kv_norm_g, mla_w_kv_a,
           mla_kv_a_norm_g, mla_w_kv_b, mla_w_q_a, mla_q_a_norm_g, mla_w_q_b, mla_w_out,
           ffn_w_up, ffn_w_down, final_norm_g):
    b, s, d = x.shape
    assert d == D_MODEL and s % ATTN_Q_TILE == 0 and s % SEQ_TILE == 0 and s % ROW_TILE == 0
    assert fox_w_in.shape[0] == 1 and mla_w_q_a.shape[0] == 1
    x2d = x.reshape(b * s, d)

    b_f = jnp.pad(fox_b_f[0].astype(F32), (0, V7X_LANES - FOX_HEADS)).reshape(1, V7X_LANES)
    q_t, k, v_t, f_logit = _fox_inproj(x2d, _row(norm_mix_g[0]), fox_w_in[0])
    ext = _gate_cumsum(f_logit.reshape(b, s, V7X_LANES), b_f)
    ctx = _fox_attn(q_t, k.reshape(b, s, FOX_WIDTH), ext, v_t)
    w_up_all = ffn_w_up.astype(BF16)
    w_down_all = ffn_w_down.astype(BF16)
    x2d = _mix_ffn(x2d, ctx.reshape(b * s, FOX_WIDTH), fox_w_out[0].astype(BF16),
                   _row(norm_ffn_g[0]), w_up_all, w_down_all, 0, _row(final_norm_g),
                   final_norm=False)

    w_kv_a = jnp.pad(mla_w_kv_a, ((0, 0), (0, V7X_LANES - QK_ROPE_DIM))).astype(BF16)
    w_kv_b = mla_w_kv_b.reshape(KV_LORA_RANK, MLA_HEADS, QK_NOPE_DIM + V_HEAD_DIM)
    w_k_b = w_kv_b[:, :, :QK_NOPE_DIM].reshape(KV_LORA_RANK, -1).astype(BF16)
    w_vt_b = w_kv_b[:, :, QK_NOPE_DIM:].reshape(KV_LORA_RANK, -1).T.astype(BF16)
    w_q_b = mla_w_q_b[0].reshape(Q_LORA_RANK, MLA_HEADS, QK_NOPE_DIM + QK_ROPE_DIM)
    w_qt_b = jnp.concatenate(
        [w_q_b[:, :, :QK_NOPE_DIM].reshape(Q_LORA_RANK, -1),
         w_q_b[:, :, QK_NOPE_DIM:].reshape(Q_LORA_RANK, -1)], axis=1).T.astype(BF16)
    q_cat_t, k_cat, v_t = _mla_proj(
        x2d, _row(kv_norm_g), _row(norm_mix_g[1]), w_kv_a, _row(mla_kv_a_norm_g), w_k_b, w_vt_b,
        mla_w_q_a[0].astype(BF16), _row(mla_q_a_norm_g[0]), w_qt_b, _rope_tables(s), s)

    cat = MLA_HEADS * MLA_QK_PAD
    ctx = _mla_attn(q_cat_t, k_cat.reshape(b, s, cat), v_t, b, s)
    out = _mix_ffn(x2d, ctx.reshape(b * s, MLA_HEADS * V_HEAD_DIM), mla_w_out[0].astype(BF16),
                   _row(norm_ffn_g[1]), w_up_all, w_down_all, 1, _row(final_norm_g),
                   final_norm=True)
    return out.reshape(b, s, d)
```

```python
import functools

import jax
import jax.numpy as jnp
import numpy as np
from jax import lax
from jax.experimental import pallas as pl
from jax.experimental.pallas import tpu as pltpu

D_MODEL = 1024
FOX_HEADS = 16
FOX_HEAD_DIM = 64
FOX_WIDTH = FOX_HEADS * FOX_HEAD_DIM
MLA_HEADS = 8
QK_NOPE_DIM = 128
QK_ROPE_DIM = 64
V_HEAD_DIM = 128
Q_LORA_RANK = 384
KV_LORA_RANK = 256
ROPE_BASE = 10000.0
D_FF = 4 * D_MODEL
EPS = 1e-6

V7X_LANES = 128
V7X_VMEM_BYTES = 64 * 1024 * 1024
BF16_SUBLANES = 16
V7X_MXU_WIDTH = 256

LOG2E = 1.4426950408889634
FOX_QSCALE = (FOX_HEAD_DIM ** -0.5) * LOG2E
MLA_QSCALE = ((QK_NOPE_DIM + QK_ROPE_DIM) ** -0.5) * LOG2E
MLA_QK_PAD = 256

ROW_TILE = 512
ATTN_Q_TILE = 1024
FF_CHUNK = 512

BF16 = jnp.bfloat16
F32 = jnp.float32
_NT = (((1,), (1,)), ((), ()))


def _compiler_params(n_grid, pipelined_bytes, resident_bytes, temp_bytes, flags=None):
    need = 2 * pipelined_bytes + resident_bytes + temp_bytes
    return pltpu.CompilerParams(
        dimension_semantics=("arbitrary",) * n_grid,
        vmem_limit_bytes=min(need, V7X_VMEM_BYTES),
        flags=flags,
    )


ATTN_FLAGS = None


def _resident(shape, index_map):
    return pl.BlockSpec(shape, index_map, pipeline_mode=pl.Buffered(1))


def _rms_unit(xf):
    return xf * lax.rsqrt(jnp.mean(xf * xf, axis=-1, keepdims=True) + EPS)


def _dot(a, b):
    return jnp.dot(a, b, preferred_element_type=F32)


def _split3(x):
    hi = x.astype(BF16)
    rem = x - hi.astype(F32)
    mid = rem.astype(BF16)
    lo = (rem - mid.astype(F32)).astype(BF16)
    return hi, mid, lo


def _pack3(x):
    heads = FOX_HEADS
    lane = lax.broadcasted_iota(jnp.int32, x.shape, 1)
    hi, mid, lo = (piece.astype(F32) for piece in _split3(x))
    packed = jnp.where(lane < heads, hi,
                       jnp.where(lane < 2 * heads, pltpu.roll(mid, heads, 1),
                                 jnp.where(lane < 3 * heads, pltpu.roll(lo, 2 * heads, 1), 0.0)))
    return packed.astype(BF16)


def _log_sigmoid_pieces(z):
    return _pack3(jnp.minimum(z, 0.0) - jnp.log1p(jnp.exp(-jnp.abs(z))))


def _decay_extension(pieces, carry_ref):
    rows = pieces.shape[0]
    heads = FOX_HEADS
    row = lax.broadcasted_iota(jnp.int32, (rows, rows), 0)
    col = lax.broadcasted_iota(jnp.int32, (rows, rows), 1)
    tri = jnp.where(row >= col, 1.0, 0.0).astype(BF16)
    sums = _dot(tri, pieces)
    c = (sums + pltpu.roll(sums, V7X_LANES - heads, 1)
         + pltpu.roll(sums, V7X_LANES - 2 * heads, 1) + carry_ref[...])
    carry_ref[...] = c[rows - 1:rows, :]
    return _pack3(c * LOG2E)


def _fox_inproj_kernel(x_ref, g_ref, w_ref, b_ref, qt_ref, k_ref, vt_ref, e_ref,
                       wk_scr, wqvt_scr, wf_scr, carry_scr, *, tiles_per_seq):
    @pl.when(pl.program_id(0) % tiles_per_seq == 0)
    def _():
        carry_scr[...] = jnp.zeros_like(carry_scr)

    @pl.when(pl.program_id(0) == 0)
    def _():
        chunk = V7X_MXU_WIDTH
        wk_scr[...] = w_ref[:, FOX_WIDTH:2 * FOX_WIDTH].astype(BF16)
        for dst, src in ((0, 0), (FOX_WIDTH, 2 * FOX_WIDTH)):
            for c in range(0, FOX_WIDTH, chunk):
                block = w_ref[:, src + c:src + c + chunk]
                wqvt_scr[dst + c:dst + c + chunk, :] = block.T.astype(BF16)
        wf_scr[...] = jnp.zeros_like(wf_scr)
        wf_scr[:, 0:FOX_HEADS] = w_ref[:, 3 * FOX_WIDTH:3 * FOX_WIDTH + FOX_HEADS].astype(BF16)

    h = (_rms_unit(x_ref[...]) * g_ref[...]).astype(BF16)
    gate_pieces = _log_sigmoid_pieces(_dot(h, wf_scr[...]) + b_ref[...])
    k_ref[...] = _dot(h, wk_scr[...]).astype(BF16)
    q_t = lax.dot_general(wqvt_scr[0:FOX_WIDTH, :], h, _NT, preferred_element_type=F32)
    qt_ref[...] = (q_t * FOX_QSCALE).astype(BF16)
    e_ref[...] = _decay_extension(gate_pieces, carry_scr)
    v_t = lax.dot_general(wqvt_scr[FOX_WIDTH:2 * FOX_WIDTH, :], h, _NT, preferred_element_type=F32)
    vt_ref[...] = v_t.astype(BF16)


def _fox_inproj(x2d, g, w_in, b_f, seq):
    m = x2d.shape[0]
    tm = ROW_TILE
    in_cols = w_in.shape[-1]
    row = lambda i: (i, 0)
    col = lambda i: (0, i)
    fixed = lambda i: (0, 0)
    feature_major = jax.ShapeDtypeStruct((FOX_WIDTH, m), BF16)
    pipelined = tm * D_MODEL * 4 + 3 * tm * FOX_WIDTH * 2 + tm * V7X_LANES * 2
    resident = D_MODEL * 4 + D_MODEL * in_cols * 4 + V7X_LANES * 4
    scratch = D_MODEL * 3 * FOX_WIDTH * 2 + D_MODEL * V7X_LANES * 2 + 8 * V7X_LANES * 4
    temps = (scratch + tm * D_MODEL * (4 + 2) + 2 * tm * FOX_WIDTH * 4
             + 2 * tm * tm * 4 + 12 * tm * V7X_LANES * 4)
    return pl.pallas_call(
        functools.partial(_fox_inproj_kernel, tiles_per_seq=seq // tm),
        grid=(m // tm,),
        in_specs=[
            pl.BlockSpec((tm, D_MODEL), row),
            _resident((1, D_MODEL), fixed),
            _resident((D_MODEL, in_cols), fixed),
            _resident((1, V7X_LANES), fixed),
        ],
        out_specs=[
            pl.BlockSpec((FOX_WIDTH, tm), col),
            pl.BlockSpec((tm, FOX_WIDTH), row),
            pl.BlockSpec((FOX_WIDTH, tm), col),
            pl.BlockSpec((tm, V7X_LANES), row),
        ],
        out_shape=[feature_major, jax.ShapeDtypeStruct((m, FOX_WIDTH), BF16), feature_major,
                   jax.ShapeDtypeStruct((m, V7X_LANES), BF16)],
        scratch_shapes=[pltpu.VMEM((D_MODEL, FOX_WIDTH), BF16),
                        pltpu.VMEM((2 * FOX_WIDTH, D_MODEL), BF16),
                        pltpu.VMEM((D_MODEL, V7X_LANES), BF16),
                        pltpu.VMEM((1, V7X_LANES), F32)],
        compiler_params=_compiler_params(1, pipelined, resident, temps),
        name="fox_inproj",
    )(x2d, g, w_in, b_f)


def _causal_attention(qi, chains, score_fn, next_tile_score_fn, value_fn,
                      s_scr, bmax_scr, m_scr, acc_scr):
    _, _, strips, tk, strip_width = s_scr.shape
    assert strip_width == V7X_MXU_WIDTH and strips * strip_width == 2 * tk
    m_scr[...] = jnp.full(m_scr.shape, -jnp.inf, F32)
    acc_scr[...] = jnp.zeros(acc_scr.shape, F32)
    ones = jnp.ones((BF16_SUBLANES, tk), BF16)

    def lanes(n):
        return slice(n * V7X_MXU_WIDTH, (n + 1) * V7X_MXU_WIDTH)

    def store_scores(s_t, slot, c, n):
        s_scr[slot, c, n] = s_t
        bmax_scr[slot, c, :, lanes(n)] = jnp.max(s_t, axis=0, keepdims=True)

    def issue(j, slot, c, n):
        store_scores(score_fn(j, c, n, tk), slot, c, n)

    def visible_keys(n, key_offset):
        return min(tk, (n + 1) * V7X_MXU_WIDTH - key_offset)

    def issue_masked(j, slot, c, n, key_offset):
        nk = visible_keys(n, key_offset)
        s_scr[slot, c, n, 0:nk, :] = score_fn(j, c, n, nk)

    def consume(j, slot, c, n, key_offset=None):
        nk = tk if key_offset is None else visible_keys(n, key_offset)
        s_t = s_scr[slot, c, n, 0:nk, :]
        if key_offset is None:
            block_max = bmax_scr[slot, c, :, lanes(n)]
        else:
            key = lax.broadcasted_iota(jnp.int32, s_t.shape, 0) + key_offset
            query = lax.broadcasted_iota(jnp.int32, s_t.shape, 1) + n * V7X_MXU_WIDTH
            s_t = jnp.where(key <= query, s_t, -jnp.inf)
            block_max = jnp.max(s_t, axis=0, keepdims=True)
        m = m_scr[c, :, lanes(n)]
        m_new = jnp.maximum(m, block_max)
        p_t = jnp.exp2(s_t - m_new).astype(BF16)
        m_scr[c, :, lanes(n)] = m_new
        v_ones = jnp.concatenate([value_fn(j, c, nk), ones[:, 0:nk]], axis=0)
        acc_scr[c, :, lanes(n)] = (jnp.exp2(m - m_new) * acc_scr[c, :, lanes(n)]
                                   + _dot(v_ones, p_t))

    units = [(c, n) for n in range(strips) for c in range(chains)]

    def consume_and_issue(j, slot):
        for c, n in units:
            consume(j, slot, c, n)
            issue(j + 1, 1 - slot, c, n)

    @pl.when(qi == 0)
    def _():
        for c, n in units:
            issue(0, 0, c, n)

    def two_blocks(i):
        consume_and_issue(2 * i, 0)
        consume_and_issue(2 * i + 1, 1)

    def four_blocks(i, carry):
        two_blocks(2 * i)
        two_blocks(2 * i + 1)
        return carry

    lax.fori_loop(0, qi // 2, four_blocks, 0)

    @pl.when(qi % 2 == 1)
    def _():
        two_blocks(qi - 1)

    half = strips // 2
    for c, n in units:
        consume(2 * qi, 0, c, n, key_offset=0 if n < half else None)
        if n >= half:
            issue_masked(2 * qi + 1, 1, c, n, tk)
        else:
            store_scores(next_tile_score_fn(c, n), 0, c, n)
    for c, n in units:
        if n >= half:
            consume(2 * qi + 1, 1, c, n, key_offset=tk)
            store_scores(next_tile_score_fn(c, n), 0, c, n)


def _attention_scratch(tq, chains, width):
    tk = tq // 2
    return [pltpu.VMEM((2, chains, tq // V7X_MXU_WIDTH, tk, V7X_MXU_WIDTH), F32),
            pltpu.VMEM((2, chains, 1, tq), F32),
            pltpu.VMEM((chains, 1, tq), F32),
            pltpu.VMEM((chains, width + BF16_SUBLANES, tq), F32)]


def _attention_scratch_bytes(tq, chains, width):
    return chains * tq * tq * 4 + chains * (3 * 8 + width + BF16_SUBLANES) * tq * 4


def _attention_output(acc_scr, chains, width):
    return jnp.concatenate(
        [acc_scr[c, 0:width, :] / acc_scr[c, width:width + 1, :] for c in range(chains)], axis=0)


FOX_HEADS_PER_STEP = 4


def _fox_attn_kernel(qt_ref, qt_next_ref, k_ref, e_ref, vt_ref, o_ref,
                     qa_scr, s_scr, bmax_scr, m_scr, acc_scr):
    n_heads = FOX_HEADS_PER_STEP
    tk = s_scr.shape[3]
    first_head = n_heads * pl.program_id(1)
    qi = pl.program_id(2)
    cur = qi % 2

    def pair_lanes(h):
        return slice((h // 2) * V7X_LANES, (h // 2 + 1) * V7X_LANES)

    def put_features(slot, ref):
        row = lax.broadcasted_iota(jnp.int32, (V7X_LANES, ref.shape[1]), 0)
        for h in range(n_heads):
            q_t = ref[pair_lanes(h), :]
            own = (row >= (h % 2) * FOX_HEAD_DIM) & (row < (h % 2 + 1) * FOX_HEAD_DIM)
            qa_scr[slot, h, 0:V7X_LANES, :] = jnp.where(own, q_t, jnp.zeros_like(q_t))

    @pl.when(qi == 0)
    def _():
        put_features(0, qt_ref)
        row = lax.broadcasted_iota(jnp.int32, (V7X_LANES, qt_ref.shape[1]), 0)
        for h in range(n_heads):
            head = first_head + h
            bias_row = ((row == head) | (row == head + FOX_HEADS) | (row == head + 2 * FOX_HEADS))
            selector = jnp.where(bias_row, -1.0, 0.0).astype(BF16)
            qa_scr[0, h, V7X_LANES:, :] = selector
            qa_scr[1, h, V7X_LANES:, :] = selector

    put_features(1 - cur, qt_next_ref)

    def keys(j, h, nk):
        ks = pl.multiple_of(j * tk, tk)
        return jnp.concatenate([k_ref[0, pl.ds(ks, nk), pair_lanes(h)],
                                e_ref[0, pl.ds(ks, nk), :]], axis=1)

    def strip(n):
        return slice(n * V7X_MXU_WIDTH, (n + 1) * V7X_MXU_WIDTH)

    def score_fn(j, h, n, nk):
        return _dot(keys(j, h, nk), qa_scr[cur, h, :, strip(n)])

    def next_tile_score_fn(h, n):
        return _dot(keys(0, h, tk), qa_scr[1 - cur, h, :, strip(n)])

    def value_fn(j, h, nk):
        rows = slice(h * FOX_HEAD_DIM, (h + 1) * FOX_HEAD_DIM)
        return vt_ref[rows, pl.ds(pl.multiple_of(j * tk, tk), nk)]

    _causal_attention(qi, n_heads, score_fn, next_tile_score_fn,
                      value_fn, s_scr, bmax_scr, m_scr, acc_scr)
    o_ref[0] = _attention_output(acc_scr, n_heads, FOX_HEAD_DIM).T.astype(o_ref.dtype)


def _fox_attn(q_t, k, ext, v_t):
    b, s, _ = k.shape
    t = ATTN_Q_TILE
    n_tiles = s // t
    n_heads = FOX_HEADS_PER_STEP
    width = n_heads * FOX_HEAD_DIM
    pipelined = 3 * t * width * 2 + 2 * s * width * 2 + s * V7X_LANES * 2
    scratch = (2 * n_heads * 2 * V7X_LANES * t * 2
               + _attention_scratch_bytes(t, n_heads, FOX_HEAD_DIM))
    temps = scratch + n_heads * (t * t // 2 * (4 + 2) + t * V7X_LANES * 4)
    return pl.pallas_call(
        _fox_attn_kernel,
        grid=(b, FOX_HEADS // n_heads, n_tiles),
        in_specs=[
            pl.BlockSpec((width, t), lambda i, p, j: (p, i * n_tiles + j)),
            pl.BlockSpec((width, t),
                         lambda i, p, j: (p, i * n_tiles + jnp.minimum(j + 1, n_tiles - 1))),
            pl.BlockSpec((1, s, width), lambda i, p, j: (i, 0, p)),
            pl.BlockSpec((1, s, V7X_LANES), lambda i, p, j: (i, 0, 0)),
            pl.BlockSpec((width, s), lambda i, p, j: (p, i)),
        ],
        out_specs=pl.BlockSpec((1, t, width), lambda i, p, j: (i, j, p)),
        out_shape=jax.ShapeDtypeStruct((b, s, FOX_WIDTH), BF16),
        scratch_shapes=[pltpu.VMEM((2, n_heads, 2 * V7X_LANES, t), BF16)]
        + _attention_scratch(t, n_heads, FOX_HEAD_DIM),
        compiler_params=_compiler_params(3, pipelined, 0, temps, ATTN_FLAGS),
        name="fox_attn",
    )(q_t, q_t, k, ext, v_t)


def _mix_ffn_kernel(x_ref, ctx_ref, wo_ref, g_ref, wu_ref, wd_ref, gf_ref, o_ref, *, final_norm):
    x1 = x_ref[...] + _dot(ctx_ref[...], wo_ref[...].astype(BF16))
    hn = (_rms_unit(x1) * g_ref[...]).astype(BF16)
    acc = x1
    for c in range(D_FF // FF_CHUNK):
        cols = slice(c * FF_CHUNK, (c + 1) * FF_CHUNK)
        up = jnp.maximum(_dot(hn, wu_ref[:, cols].astype(BF16)), 0.0)
        acc = acc + _dot((up * up).astype(BF16), wd_ref[cols, :].astype(BF16))
    if final_norm:
        acc = _rms_unit(acc) * gf_ref[...]
    o_ref[...] = acc


def _mix_ffn(x2d, ctx2d, w_out, g_ffn, w_up_all, w_down_all, layer, g_final, final_norm):
    m = x2d.shape[0]
    tm = ROW_TILE
    row = lambda i: (i, 0)
    fixed = lambda i: (0, 0)
    of_layer = lambda i: (layer, 0, 0)
    pipelined = tm * D_MODEL * (4 + 2 + 4)
    resident = (D_MODEL * D_MODEL + 2 * D_MODEL * D_FF) * 4 + 2 * D_MODEL * 4
    temps = (tm * D_MODEL * (4 + 4 + 2) + tm * FF_CHUNK * (4 + 4 + 2)
             + 2 * D_MODEL * FF_CHUNK * 2 + D_MODEL * D_MODEL * 2)
    return pl.pallas_call(
        functools.partial(_mix_ffn_kernel, final_norm=final_norm),
        grid=(m // tm,),
        in_specs=[
            pl.BlockSpec((tm, D_MODEL), row),
            pl.BlockSpec((tm, D_MODEL), row),
            _resident((D_MODEL, D_MODEL), fixed),
            _resident((1, D_MODEL), fixed),
            _resident((None, D_MODEL, D_FF), of_layer),
            _resident((None, D_FF, D_MODEL), of_layer),
            _resident((1, D_MODEL), fixed),
        ],
        out_specs=pl.BlockSpec((tm, D_MODEL), row),
        out_shape=jax.ShapeDtypeStruct((m, D_MODEL), F32),
        compiler_params=_compiler_params(1, pipelined, resident, temps),
        name="mix_ffn_final" if final_norm else "mix_ffn",
    )(x2d, ctx2d, w_out, g_ffn, w_up_all, w_down_all, g_final)


def _rope_lanes(t, cos, sin_signed):
    lane = lax.broadcasted_iota(jnp.int32, t.shape, 1)
    low = (lane % QK_ROPE_DIM) < (QK_ROPE_DIM // 2)
    partner = jnp.where(low,
                        pltpu.roll(t, V7X_LANES - QK_ROPE_DIM // 2, 1),
                        pltpu.roll(t, QK_ROPE_DIM // 2, 1))
    return t * cos + partner * sin_signed


def _rope_rows(t, cos_t, sin_signed_t):
    half = QK_ROPE_DIM // 2
    partner = jnp.concatenate([t[half:, :], t[:half, :]], axis=0)
    return t * cos_t + partner * sin_signed_t


def _mla_proj_kernel(x_ref, gkv_ref, gq_ref, wkva_ref, gkva_ref, wkb_ref, wvt_ref, wqa_ref,
                     gqa_ref, wqt_ref, cos_ref, sin_ref, cost_ref, sint_ref,
                     qcat_t_ref, kcat_ref, vt_ref):
    unit = _rms_unit(x_ref[...])
    tm = unit.shape[0]
    nope_all = MLA_HEADS * QK_NOPE_DIM

    src = (unit * gkv_ref[...]).astype(BF16)
    kv_a = _dot(src, wkva_ref[...])
    c_kv = (_rms_unit(kv_a[:, 0:KV_LORA_RANK]) * gkva_ref[...]).astype(BF16)
    k_rope = _rope_lanes(kv_a[:, KV_LORA_RANK:KV_LORA_RANK + V7X_LANES],
                         cos_ref[...], sin_ref[...])
    k_rope_even = k_rope.astype(BF16)
    k_rope_odd = pltpu.roll(k_rope, QK_ROPE_DIM, 1).astype(BF16)
    k_nope = _dot(c_kv, wkb_ref[...]).astype(BF16)
    vt_ref[...] = lax.dot_general(wvt_ref[...], c_kv, _NT, preferred_element_type=F32).astype(BF16)
    for head in range(MLA_HEADS):
        lo = head * MLA_QK_PAD
        kcat_ref[:, lo:lo + QK_NOPE_DIM] = k_nope[:, head * QK_NOPE_DIM:(head + 1) * QK_NOPE_DIM]
        kcat_ref[:, lo + QK_NOPE_DIM:lo + MLA_QK_PAD] = k_rope_odd if head % 2 else k_rope_even

    h = (unit * gq_ref[...]).astype(BF16)
    c_q = (_rms_unit(_dot(h, wqa_ref[...])) * gqa_ref[...]).astype(BF16)
    q_nope_t = lax.dot_general(wqt_ref[0:nope_all, :], c_q, _NT, preferred_element_type=F32)
    q_rope_t = lax.dot_general(wqt_ref[nope_all:, :], c_q, _NT, preferred_element_type=F32)
    cos_t = cost_ref[...]
    sin_t = sint_ref[...]
    zero = jnp.zeros((QK_ROPE_DIM, tm), BF16)
    for head in range(MLA_HEADS):
        lo = head * MLA_QK_PAD
        nope = q_nope_t[head * QK_NOPE_DIM:(head + 1) * QK_NOPE_DIM, :]
        rope = _rope_rows(q_rope_t[head * QK_ROPE_DIM:(head + 1) * QK_ROPE_DIM, :], cos_t, sin_t)
        rope = (rope * MLA_QSCALE).astype(BF16)
        qcat_t_ref[lo:lo + QK_NOPE_DIM, :] = (nope * MLA_QSCALE).astype(BF16)
        qcat_t_ref[lo + QK_NOPE_DIM:lo + MLA_QK_PAD, :] = jnp.concatenate(
            [zero, rope] if head % 2 else [rope, zero], axis=0)


def _mla_proj(x2d, g_kv, g_q, w_kv_a, g_kv_a, w_k_b, w_vt_b, w_q_a, g_q_a, w_qt_b, rope, seq):
    m = x2d.shape[0]
    tm = ROW_TILE
    row = lambda i: (i, 0)
    col = lambda i: (0, i)
    fixed = lambda i: (0, 0)
    pos = lambda i: (i % (seq // tm), 0)
    pos_t = lambda i: (0, i % (seq // tm))
    kva_cols = w_kv_a.shape[1]
    cat = MLA_HEADS * MLA_QK_PAD
    pipelined = tm * D_MODEL * 4 + 2 * tm * V7X_LANES * 4 + tm * (2 * cat + D_MODEL) * 2
    resident = (D_MODEL * kva_cols + KV_LORA_RANK * 2 * D_MODEL + D_MODEL * Q_LORA_RANK
                + Q_LORA_RANK * 3 * D_MODEL // 2) * 2 + 4 * D_MODEL * 4
    temps = tm * D_MODEL * (4 + 2 + 2) + 4 * tm * D_MODEL * 4
    return pl.pallas_call(
        _mla_proj_kernel,
        grid=(m // tm,),
        in_specs=[
            pl.BlockSpec((tm, D_MODEL), row),
            _resident((1, D_MODEL), fixed),
            _resident((1, D_MODEL), fixed),
            _resident((D_MODEL, kva_cols), fixed),
            _resident((1, KV_LORA_RANK), fixed),
            _resident((KV_LORA_RANK, MLA_HEADS * QK_NOPE_DIM), fixed),
            _resident((MLA_HEADS * V_HEAD_DIM, KV_LORA_RANK), fixed),
            _resident((D_MODEL, Q_LORA_RANK), fixed),
            _resident((1, Q_LORA_RANK), fixed),
            _resident((3 * D_MODEL // 2, Q_LORA_RANK), fixed),
            pl.BlockSpec((tm, V7X_LANES), pos),
            pl.BlockSpec((tm, V7X_LANES), pos),
            pl.BlockSpec((QK_ROPE_DIM, tm), pos_t),
            pl.BlockSpec((QK_ROPE_DIM, tm), pos_t),
        ],
        out_specs=[
            pl.BlockSpec((cat, tm), col),
            pl.BlockSpec((tm, cat), row),
            pl.BlockSpec((MLA_HEADS * V_HEAD_DIM, tm), col),
        ],
        out_shape=[
            jax.ShapeDtypeStruct((cat, m), BF16),
            jax.ShapeDtypeStruct((m, cat), BF16),
            jax.ShapeDtypeStruct((MLA_HEADS * V_HEAD_DIM, m), BF16),
        ],
        compiler_params=_compiler_params(1, pipelined, resident, temps),
        name="mla_proj",
    )(x2d, g_kv, g_q, w_kv_a, g_kv_a, w_k_b, w_vt_b, w_q_a, g_q_a, w_qt_b, *rope)


MLA_HEADS_PER_STEP = 2


def _mla_attn_kernel(qt_ref, qt_next_ref, k_ref, vt_ref, o_ref, s_scr, bmax_scr, m_scr, acc_scr):
    n = MLA_HEADS_PER_STEP
    tk = s_scr.shape[3]

    def scores(q_ref, j, h, strip, nk):
        feats = slice(h * MLA_QK_PAD, (h + 1) * MLA_QK_PAD)
        k = k_ref[0, pl.ds(pl.multiple_of(j * tk, tk), nk), feats]
        return _dot(k, q_ref[feats, strip * V7X_MXU_WIDTH:(strip + 1) * V7X_MXU_WIDTH])

    def value_fn(j, h, nk):
        rows = slice(h * V_HEAD_DIM, (h + 1) * V_HEAD_DIM)
        return vt_ref[rows, pl.ds(pl.multiple_of(j * tk, tk), nk)]

    _causal_attention(pl.program_id(2), n,
                      functools.partial(scores, qt_ref),
                      lambda h, strip: scores(qt_next_ref, 0, h, strip, tk),
                      value_fn, s_scr, bmax_scr, m_scr, acc_scr)
    o_ref[0] = _attention_output(acc_scr, n, V_HEAD_DIM).T.astype(o_ref.dtype)


def _mla_attn(q_cat_t, k_cat, v_t, b, s):
    t = ATTN_Q_TILE
    n_tiles = s // t
    n = MLA_HEADS_PER_STEP
    pipelined = n * (2 * t * MLA_QK_PAD * 2 + s * MLA_QK_PAD * 2 + s * V_HEAD_DIM * 2
                     + t * V_HEAD_DIM * 2)
    temps = (_attention_scratch_bytes(t, n, V_HEAD_DIM)
             + n * (t * t // 2 * (4 + 2) + t * V_HEAD_DIM * 4))
    return pl.pallas_call(
        _mla_attn_kernel,
        grid=(b, MLA_HEADS // n, n_tiles),
        in_specs=[
            pl.BlockSpec((n * MLA_QK_PAD, t), lambda i, h, j: (h, i * n_tiles + j)),
            pl.BlockSpec((n * MLA_QK_PAD, t),
                         lambda i, h, j: (h, i * n_tiles + jnp.minimum(j + 1, n_tiles - 1))),
            pl.BlockSpec((1, s, n * MLA_QK_PAD), lambda i, h, j: (i, 0, h)),
            pl.BlockSpec((n * V_HEAD_DIM, s), lambda i, h, j: (h, i)),
        ],
        out_specs=pl.BlockSpec((1, t, n * V_HEAD_DIM), lambda i, h, j: (i, j, h)),
        out_shape=jax.ShapeDtypeStruct((b, s, MLA_HEADS * V_HEAD_DIM), BF16),
        scratch_shapes=_attention_scratch(t, n, V_HEAD_DIM),
        compiler_params=_compiler_params(3, pipelined, 0, temps, ATTN_FLAGS),
        name="mla_attn",
    )(q_cat_t, q_cat_t, k_cat, v_t)


def _rope_tables(seq):
    f32 = np.float32
    inv = (f32(1.0) / (f32(ROPE_BASE) ** (np.arange(0, QK_ROPE_DIM, 2, dtype=f32)
                                          / f32(QK_ROPE_DIM)))).astype(f32)
    ang = np.arange(seq, dtype=f32)[:, None] * inv[None, :]
    cos, sin = np.cos(ang), np.sin(ang)
    cos_head = np.concatenate([cos, cos], axis=1)
    sin_head = np.concatenate([-sin, sin], axis=1)
    reps = V7X_LANES // QK_ROPE_DIM
    tables = (np.tile(cos_head, (1, reps)), np.tile(sin_head, (1, reps)), cos_head.T, sin_head.T)
    return tuple(jnp.asarray(np.ascontiguousarray(t), dtype=F32) for t in tables)


def _row(vec):
    return vec.reshape(1, -1).astype(F32)


def kernel(x, norm_mix_g, norm_ffn_g, fox_w_in, fox_b_f, fox_w_out, kv_norm_g, mla_w_kv_a,
           mla_kv_a_norm_g, mla_w_kv_b, mla_w_q_a, mla_q_a_norm_g, mla_w_q_b, mla_w_out,
           ffn_w_up, ffn_w_down, final_norm_g):
    b, s, d = x.shape
    assert d == D_MODEL and s % ATTN_Q_TILE == 0 and s % ROW_TILE == 0
    assert fox_w_in.shape[0] == 1 and mla_w_q_a.shape[0] == 1
    x2d = x.reshape(b * s, d)

    b_f = jnp.pad(fox_b_f[0].astype(F32), (0, V7X_LANES - FOX_HEADS)).reshape(1, V7X_LANES)
    q_t, k, v_t, ext = _fox_inproj(x2d, _row(norm_mix_g[0]), fox_w_in[0], b_f, s)
    ext = ext.reshape(b, s, V7X_LANES)
    ctx = _fox_attn(q_t, k.reshape(b, s, FOX_WIDTH), ext, v_t)
    x2d = _mix_ffn(x2d, ctx.reshape(b * s, FOX_WIDTH), fox_w_out[0],
                   _row(norm_ffn_g[0]), ffn_w_up, ffn_w_down, 0, _row(final_norm_g),
                   final_norm=False)

    w_kv_a = jnp.pad(mla_w_kv_a, ((0, 0), (0, V7X_LANES - QK_ROPE_DIM))).astype(BF16)
    w_kv_b = mla_w_kv_b.reshape(KV_LORA_RANK, MLA_HEADS, QK_NOPE_DIM + V_HEAD_DIM)
    w_k_b = w_kv_b[:, :, :QK_NOPE_DIM].reshape(KV_LORA_RANK, -1).astype(BF16)
    w_vt_b = w_kv_b[:, :, QK_NOPE_DIM:].reshape(KV_LORA_RANK, -1).T.astype(BF16)
    w_q_b = mla_w_q_b[0].reshape(Q_LORA_RANK, MLA_HEADS, QK_NOPE_DIM + QK_ROPE_DIM)
    w_qt_b = jnp.concatenate(
        [w_q_b[:, :, :QK_NOPE_DIM].reshape(Q_LORA_RANK, -1),
         w_q_b[:, :, QK_NOPE_DIM:].reshape(Q_LORA_RANK, -1)], axis=1).T.astype(BF16)
    q_cat_t, k_cat, v_t = _mla_proj(
        x2d, _row(kv_norm_g), _row(norm_mix_g[1]), w_kv_a, _row(mla_kv_a_norm_g), w_k_b, w_vt_b,
        mla_w_q_a[0].astype(BF16), _row(mla_q_a_norm_g[0]), w_qt_b, _rope_tables(s), s)

    cat = MLA_HEADS * MLA_QK_PAD
    ctx = _mla_attn(q_cat_t, k_cat.reshape(b, s, cat), v_t, b, s)
    out = _mix_ffn(x2d, ctx.reshape(b * s, MLA_HEADS * V_HEAD_DIM), mla_w_out[0],
                   _row(norm_ffn_g[1]), ffn_w_up, ffn_w_down, 1, _row(final_norm_g),
                   final_norm=True)
    return out.reshape(b, s, d)
```

```python
import functools

import jax
import jax.numpy as jnp
import numpy as np
from jax import lax
from jax.experimental import pallas as pl
from jax.experimental.pallas import tpu as pltpu

D_MODEL = 1024
FOX_HEADS = 16
FOX_HEAD_DIM = 64
FOX_WIDTH = FOX_HEADS * FOX_HEAD_DIM
MLA_HEADS = 8
QK_NOPE_DIM = 128
QK_ROPE_DIM = 64
V_HEAD_DIM = 128
Q_LORA_RANK = 384
KV_LORA_RANK = 256
ROPE_BASE = 10000.0
D_FF = 4 * D_MODEL
EPS = 1e-6

V7X_LANES = 128
V7X_VMEM_BYTES = 64 * 1024 * 1024
BF16_SUBLANES = 16
V7X_MXU_WIDTH = 256

LOG2E = 1.4426950408889634
FOX_QSCALE = (FOX_HEAD_DIM ** -0.5) * LOG2E
MLA_QSCALE = ((QK_NOPE_DIM + QK_ROPE_DIM) ** -0.5) * LOG2E
MLA_QK_PAD = 256

ROW_TILE = 512
ATTN_Q_TILE = 1024
FF_CHUNK = 512

BF16 = jnp.bfloat16
F32 = jnp.float32
_NT = (((1,), (1,)), ((), ()))


def _compiler_params(n_grid, pipelined_bytes, resident_bytes, temp_bytes):
    need = 2 * pipelined_bytes + resident_bytes + temp_bytes
    return pltpu.CompilerParams(
        dimension_semantics=("arbitrary",) * n_grid,
        vmem_limit_bytes=min(need, V7X_VMEM_BYTES),
    )


def _resident(shape, index_map):
    return pl.BlockSpec(shape, index_map, pipeline_mode=pl.Buffered(1))


def _rms_unit(xf):
    return xf * lax.rsqrt(jnp.mean(xf * xf, axis=-1, keepdims=True) + EPS)


def _dot(a, b):
    return jnp.dot(a, b, preferred_element_type=F32)


def _split3(x):
    hi = x.astype(BF16)
    rem = x - hi.astype(F32)
    mid = rem.astype(BF16)
    lo = (rem - mid.astype(F32)).astype(BF16)
    return hi, mid, lo


def _pack3(x):
    heads = FOX_HEADS
    lane = lax.broadcasted_iota(jnp.int32, x.shape, 1)
    hi, mid, lo = (piece.astype(F32) for piece in _split3(x))
    packed = jnp.where(lane < heads, hi,
                       jnp.where(lane < 2 * heads, pltpu.roll(mid, heads, 1),
                                 jnp.where(lane < 3 * heads, pltpu.roll(lo, 2 * heads, 1), 0.0)))
    return packed.astype(BF16)


def _log_sigmoid_pieces(z):
    return _pack3(jnp.minimum(z, 0.0) - jnp.log1p(jnp.exp(-jnp.abs(z))))


def _decay_extension(pieces, carry_ref):
    rows = pieces.shape[0]
    heads = FOX_HEADS
    row = lax.broadcasted_iota(jnp.int32, (rows, rows), 0)
    col = lax.broadcasted_iota(jnp.int32, (rows, rows), 1)
    tri = jnp.where(row >= col, 1.0, 0.0).astype(BF16)
    sums = _dot(tri, pieces)
    c = (sums + pltpu.roll(sums, V7X_LANES - heads, 1)
         + pltpu.roll(sums, V7X_LANES - 2 * heads, 1) + carry_ref[...])
    carry_ref[...] = c[rows - 1:rows, :]
    return _pack3(c * LOG2E)


def _fox_inproj_kernel(x_ref, g_ref, w_ref, b_ref, qt_ref, k_ref, vt_ref, e_ref,
                       wk_scr, wqvt_scr, wf_scr, carry_scr, *, tiles_per_seq):
    @pl.when(pl.program_id(0) % tiles_per_seq == 0)
    def _():
        carry_scr[...] = jnp.zeros_like(carry_scr)

    @pl.when(pl.program_id(0) == 0)
    def _():
        chunk = V7X_MXU_WIDTH
        wqvt_scr[0:FOX_WIDTH, :] = w_ref[0:FOX_WIDTH, :].astype(BF16)
        wqvt_scr[FOX_WIDTH:, :] = w_ref[2 * FOX_WIDTH:3 * FOX_WIDTH, :].astype(BF16)
        for c in range(0, FOX_WIDTH, chunk):
            block = w_ref[FOX_WIDTH + c:FOX_WIDTH + c + chunk, :]
            wk_scr[:, c:c + chunk] = block.T.astype(BF16)
        gate_rows = jnp.concatenate(
            [w_ref[3 * FOX_WIDTH:3 * FOX_WIDTH + FOX_HEADS, :],
             jnp.zeros((V7X_LANES - FOX_HEADS, D_MODEL), F32)], axis=0)
        wf_scr[...] = gate_rows.T.astype(BF16)

    h = (_rms_unit(x_ref[...]) * g_ref[...]).astype(BF16)
    gate_pieces = _log_sigmoid_pieces(_dot(h, wf_scr[...]) + b_ref[...])
    k_ref[...] = _dot(h, wk_scr[...]).astype(BF16)
    q_t = lax.dot_general(wqvt_scr[0:FOX_WIDTH, :], h, _NT, preferred_element_type=F32)
    qt_ref[...] = (q_t * FOX_QSCALE).astype(BF16)
    e_ref[...] = _decay_extension(gate_pieces, carry_scr)
    v_t = lax.dot_general(wqvt_scr[FOX_WIDTH:2 * FOX_WIDTH, :], h, _NT, preferred_element_type=F32)
    vt_ref[...] = v_t.astype(BF16)


def _fox_inproj(x2d, g, w_in_t, b_f, seq):
    m = x2d.shape[0]
    tm = ROW_TILE
    in_cols = w_in_t.shape[0]
    row = lambda i: (i, 0)
    col = lambda i: (0, i)
    fixed = lambda i: (0, 0)
    feature_major = jax.ShapeDtypeStruct((FOX_WIDTH, m), BF16)
    pipelined = tm * D_MODEL * 4 + 3 * tm * FOX_WIDTH * 2 + tm * V7X_LANES * 2
    resident = D_MODEL * 4 + D_MODEL * in_cols * 4 + V7X_LANES * 4
    scratch = D_MODEL * 3 * FOX_WIDTH * 2 + D_MODEL * V7X_LANES * 2 + 8 * V7X_LANES * 4
    temps = (scratch + tm * D_MODEL * (4 + 2) + 2 * tm * FOX_WIDTH * 4
             + 2 * tm * tm * 4 + 12 * tm * V7X_LANES * 4)
    return pl.pallas_call(
        functools.partial(_fox_inproj_kernel, tiles_per_seq=seq // tm),
        grid=(m // tm,),
        in_specs=[
            pl.BlockSpec((tm, D_MODEL), row),
            _resident((1, D_MODEL), fixed),
            _resident((in_cols, D_MODEL), fixed),
            _resident((1, V7X_LANES), fixed),
        ],
        out_specs=[
            pl.BlockSpec((FOX_WIDTH, tm), col),
            pl.BlockSpec((tm, FOX_WIDTH), row),
            pl.BlockSpec((FOX_WIDTH, tm), col),
            pl.BlockSpec((tm, V7X_LANES), row),
        ],
        out_shape=[feature_major, jax.ShapeDtypeStruct((m, FOX_WIDTH), BF16), feature_major,
                   jax.ShapeDtypeStruct((m, V7X_LANES), BF16)],
        scratch_shapes=[pltpu.VMEM((D_MODEL, FOX_WIDTH), BF16),
                        pltpu.VMEM((2 * FOX_WIDTH, D_MODEL), BF16),
                        pltpu.VMEM((D_MODEL, V7X_LANES), BF16),
                        pltpu.VMEM((1, V7X_LANES), F32)],
        compiler_params=_compiler_params(1, pipelined, resident, temps),
        name="fox_inproj",
    )(x2d, g, w_in_t, b_f)


def _causal_attention(qi, chains, score_fn, next_tile_score_fn, value_fn,
                      s_scr, bmax_scr, m_scr, acc_scr):
    _, _, strips, tk, strip_width = s_scr.shape
    assert strip_width == V7X_MXU_WIDTH and strips * strip_width == 2 * tk
    m_scr[...] = jnp.full(m_scr.shape, -jnp.inf, F32)
    acc_scr[...] = jnp.zeros(acc_scr.shape, F32)
    ones = jnp.ones((BF16_SUBLANES, tk), BF16)

    def lanes(n):
        return slice(n * V7X_MXU_WIDTH, (n + 1) * V7X_MXU_WIDTH)

    def store_scores(s_t, slot, c, n):
        s_scr[slot, c, n] = s_t
        bmax_scr[slot, c, :, lanes(n)] = jnp.max(s_t, axis=0, keepdims=True)

    def issue(j, slot, c, n):
        store_scores(score_fn(j, c, n, tk), slot, c, n)

    def visible_keys(n, key_offset):
        return min(tk, (n + 1) * V7X_MXU_WIDTH - key_offset)

    def issue_masked(j, slot, c, n, key_offset):
        nk = visible_keys(n, key_offset)
        s_scr[slot, c, n, 0:nk, :] = score_fn(j, c, n, nk)

    def consume(j, slot, c, n, key_offset=None):
        nk = tk if key_offset is None else visible_keys(n, key_offset)
        s_t = s_scr[slot, c, n, 0:nk, :]
        if key_offset is None:
            block_max = bmax_scr[slot, c, :, lanes(n)]
        else:
            key = lax.broadcasted_iota(jnp.int32, s_t.shape, 0) + key_offset
            query = lax.broadcasted_iota(jnp.int32, s_t.shape, 1) + n * V7X_MXU_WIDTH
            s_t = jnp.where(key <= query, s_t, -jnp.inf)
            block_max = jnp.max(s_t, axis=0, keepdims=True)
        m = m_scr[c, :, lanes(n)]
        m_new = jnp.maximum(m, block_max)
        p_t = jnp.exp2(s_t - m_new).astype(BF16)
        m_scr[c, :, lanes(n)] = m_new
        v_ones = jnp.concatenate([value_fn(j, c, nk), ones[:, 0:nk]], axis=0)
        acc_scr[c, :, lanes(n)] = (jnp.exp2(m - m_new) * acc_scr[c, :, lanes(n)]
                                   + _dot(v_ones, p_t))

    units = [(c, n) for n in range(strips) for c in range(chains)]

    def consume_and_issue(j, slot):
        for c, n in units:
            consume(j, slot, c, n)
            issue(j + 1, 1 - slot, c, n)

    @pl.when(qi == 0)
    def _():
        for c, n in units:
            issue(0, 0, c, n)

    def two_blocks(i):
        consume_and_issue(2 * i, 0)
        consume_and_issue(2 * i + 1, 1)

    def four_blocks(i, carry):
        two_blocks(2 * i)
        two_blocks(2 * i + 1)
        return carry

    lax.fori_loop(0, qi // 2, four_blocks, 0)

    @pl.when(qi % 2 == 1)
    def _():
        two_blocks(qi - 1)

    half = strips // 2
    for c, n in units:
        consume(2 * qi, 0, c, n, key_offset=0 if n < half else None)
        if n >= half:
            issue_masked(2 * qi + 1, 1, c, n, tk)
        else:
            store_scores(next_tile_score_fn(c, n), 0, c, n)
    for c, n in units:
        if n >= half:
            consume(2 * qi + 1, 1, c, n, key_offset=tk)
            store_scores(next_tile_score_fn(c, n), 0, c, n)


def _attention_scratch(tq, chains, width):
    tk = tq // 2
    return [pltpu.VMEM((2, chains, tq // V7X_MXU_WIDTH, tk, V7X_MXU_WIDTH), F32),
            pltpu.VMEM((2, chains, 1, tq), F32),
            pltpu.VMEM((chains, 1, tq), F32),
            pltpu.VMEM((chains, width + BF16_SUBLANES, tq), F32)]


def _attention_scratch_bytes(tq, chains, width):
    return chains * tq * tq * 4 + chains * (3 * 8 + width + BF16_SUBLANES) * tq * 4


def _attention_output(acc_scr, chains, width):
    return jnp.concatenate(
        [acc_scr[c, 0:width, :] / acc_scr[c, width:width + 1, :] for c in range(chains)], axis=0)


FOX_HEADS_PER_STEP = 4


def _fox_attn_kernel(qt_ref, qt_next_ref, k_ref, e_ref, vt_ref, o_ref,
                     qa_scr, s_scr, bmax_scr, m_scr, acc_scr):
    n_heads = FOX_HEADS_PER_STEP
    tk = s_scr.shape[3]
    first_head = n_heads * pl.program_id(1)
    qi = pl.program_id(2)
    cur = qi % 2

    def pair_lanes(h):
        return slice((h // 2) * V7X_LANES, (h // 2 + 1) * V7X_LANES)

    def put_features(slot, ref):
        row = lax.broadcasted_iota(jnp.int32, (V7X_LANES, ref.shape[1]), 0)
        for h in range(n_heads):
            q_t = ref[pair_lanes(h), :]
            own = (row >= (h % 2) * FOX_HEAD_DIM) & (row < (h % 2 + 1) * FOX_HEAD_DIM)
            qa_scr[slot, h, 0:V7X_LANES, :] = jnp.where(own, q_t, jnp.zeros_like(q_t))

    @pl.when(qi == 0)
    def _():
        put_features(0, qt_ref)
        row = lax.broadcasted_iota(jnp.int32, (V7X_LANES, qt_ref.shape[1]), 0)
        for h in range(n_heads):
            head = first_head + h
            bias_row = ((row == head) | (row == head + FOX_HEADS) | (row == head + 2 * FOX_HEADS))
            selector = jnp.where(bias_row, -1.0, 0.0).astype(BF16)
            qa_scr[0, h, V7X_LANES:, :] = selector
            qa_scr[1, h, V7X_LANES:, :] = selector

    put_features(1 - cur, qt_next_ref)

    def keys(j, h, nk):
        ks = pl.multiple_of(j * tk, tk)
        return jnp.concatenate([k_ref[0, pl.ds(ks, nk), pair_lanes(h)],
                                e_ref[0, pl.ds(ks, nk), :]], axis=1)

    def strip(n):
        return slice(n * V7X_MXU_WIDTH, (n + 1) * V7X_MXU_WIDTH)

    def score_fn(j, h, n, nk):
        return _dot(keys(j, h, nk), qa_scr[cur, h, :, strip(n)])

    def next_tile_score_fn(h, n):
        return _dot(keys(0, h, tk), qa_scr[1 - cur, h, :, strip(n)])

    def value_fn(j, h, nk):
        rows = slice(h * FOX_HEAD_DIM, (h + 1) * FOX_HEAD_DIM)
        return vt_ref[rows, pl.ds(pl.multiple_of(j * tk, tk), nk)]

    _causal_attention(qi, n_heads, score_fn, next_tile_score_fn,
                      value_fn, s_scr, bmax_scr, m_scr, acc_scr)
    o_ref[0] = _attention_output(acc_scr, n_heads, FOX_HEAD_DIM).T.astype(o_ref.dtype)


def _fox_attn(q_t, k, ext, v_t):
    b, s, _ = k.shape
    t = ATTN_Q_TILE
    n_tiles = s // t
    n_heads = FOX_HEADS_PER_STEP
    width = n_heads * FOX_HEAD_DIM
    pipelined = 3 * t * width * 2 + 2 * s * width * 2 + s * V7X_LANES * 2
    scratch = (2 * n_heads * 2 * V7X_LANES * t * 2
               + _attention_scratch_bytes(t, n_heads, FOX_HEAD_DIM))
    temps = scratch + n_heads * (t * t // 2 * (4 + 2) + t * V7X_LANES * 4)
    return pl.pallas_call(
        _fox_attn_kernel,
        grid=(b, FOX_HEADS // n_heads, n_tiles),
        in_specs=[
            pl.BlockSpec((width, t), lambda i, p, j: (p, i * n_tiles + j)),
            pl.BlockSpec((width, t),
                         lambda i, p, j: (p, i * n_tiles + jnp.minimum(j + 1, n_tiles - 1))),
            pl.BlockSpec((1, s, width), lambda i, p, j: (i, 0, p)),
            pl.BlockSpec((1, s, V7X_LANES), lambda i, p, j: (i, 0, 0)),
            pl.BlockSpec((width, s), lambda i, p, j: (p, i)),
        ],
        out_specs=pl.BlockSpec((1, t, width), lambda i, p, j: (i, j, p)),
        out_shape=jax.ShapeDtypeStruct((b, s, FOX_WIDTH), BF16),
        scratch_shapes=[pltpu.VMEM((2, n_heads, 2 * V7X_LANES, t), BF16)]
        + _attention_scratch(t, n_heads, FOX_HEAD_DIM),
        compiler_params=_compiler_params(3, pipelined, 0, temps),
        name="fox_attn",
    )(q_t, q_t, k, ext, v_t)


def _mix_ffn_kernel(x_ref, ctx_ref, wo_ref, g_ref, wu_ref, wd_ref, gf_ref, o_ref, *, final_norm):
    x1 = x_ref[...] + _dot(ctx_ref[...], wo_ref[...].astype(BF16))
    hn = (_rms_unit(x1) * g_ref[...]).astype(BF16)
    acc = x1
    for c in range(D_FF // FF_CHUNK):
        cols = slice(c * FF_CHUNK, (c + 1) * FF_CHUNK)
        up = jnp.maximum(_dot(hn, wu_ref[:, cols].astype(BF16)), 0.0)
        acc = acc + _dot((up * up).astype(BF16), wd_ref[cols, :].astype(BF16))
    if final_norm:
        acc = _rms_unit(acc) * gf_ref[...]
    o_ref[...] = acc


def _mix_ffn(x2d, ctx2d, w_out, g_ffn, w_up_all, w_down_all, layer, g_final, final_norm):
    m = x2d.shape[0]
    tm = ROW_TILE
    row = lambda i: (i, 0)
    fixed = lambda i: (0, 0)
    of_layer = lambda i: (layer, 0, 0)
    pipelined = tm * D_MODEL * (4 + 2 + 4)
    resident = (D_MODEL * D_MODEL + 2 * D_MODEL * D_FF) * 4 + 2 * D_MODEL * 4
    temps = (tm * D_MODEL * (4 + 4 + 2) + tm * FF_CHUNK * (4 + 4 + 2)
             + 2 * D_MODEL * FF_CHUNK * 2 + D_MODEL * D_MODEL * 2)
    return pl.pallas_call(
        functools.partial(_mix_ffn_kernel, final_norm=final_norm),
        grid=(m // tm,),
        in_specs=[
            pl.BlockSpec((tm, D_MODEL), row),
            pl.BlockSpec((tm, D_MODEL), row),
            _resident((D_MODEL, D_MODEL), fixed),
            _resident((1, D_MODEL), fixed),
            _resident((None, D_MODEL, D_FF), of_layer),
            _resident((None, D_FF, D_MODEL), of_layer),
            _resident((1, D_MODEL), fixed),
        ],
        out_specs=pl.BlockSpec((tm, D_MODEL), row),
        out_shape=jax.ShapeDtypeStruct((m, D_MODEL), F32),
        compiler_params=_compiler_params(1, pipelined, resident, temps),
        name="mix_ffn_final" if final_norm else "mix_ffn",
    )(x2d, ctx2d, w_out, g_ffn, w_up_all, w_down_all, g_final)


def _rope_lanes(t, cos, sin_signed):
    lane = lax.broadcasted_iota(jnp.int32, t.shape, 1)
    low = (lane % QK_ROPE_DIM) < (QK_ROPE_DIM // 2)
    partner = jnp.where(low,
                        pltpu.roll(t, V7X_LANES - QK_ROPE_DIM // 2, 1),
                        pltpu.roll(t, QK_ROPE_DIM // 2, 1))
    return t * cos + partner * sin_signed


def _rope_rows(t, cos_t, sin_signed_t):
    half = QK_ROPE_DIM // 2
    partner = jnp.concatenate([t[half:, :], t[:half, :]], axis=0)
    return t * cos_t + partner * sin_signed_t


def _mla_proj_kernel(x_ref, gkv_ref, gq_ref, wkva_ref, gkva_ref, wkb_ref, wvt_ref, wqa_ref,
                     gqa_ref, wqt_ref, cos_ref, sin_ref, cost_ref, sint_ref,
                     qcat_t_ref, kcat_ref, vt_ref):
    unit = _rms_unit(x_ref[...])
    tm = unit.shape[0]
    nope_all = MLA_HEADS * QK_NOPE_DIM

    src = (unit * gkv_ref[...]).astype(BF16)
    kv_a = _dot(src, wkva_ref[...])
    c_kv = (_rms_unit(kv_a[:, 0:KV_LORA_RANK]) * gkva_ref[...]).astype(BF16)
    k_rope = _rope_lanes(kv_a[:, KV_LORA_RANK:KV_LORA_RANK + V7X_LANES],
                         cos_ref[...], sin_ref[...])
    k_rope_even = k_rope.astype(BF16)
    k_rope_odd = pltpu.roll(k_rope, QK_ROPE_DIM, 1).astype(BF16)
    k_nope = _dot(c_kv, wkb_ref[...]).astype(BF16)
    vt_ref[...] = lax.dot_general(wvt_ref[...], c_kv, _NT, preferred_element_type=F32).astype(BF16)
    for head in range(MLA_HEADS):
        lo = head * MLA_QK_PAD
        kcat_ref[:, lo:lo + QK_NOPE_DIM] = k_nope[:, head * QK_NOPE_DIM:(head + 1) * QK_NOPE_DIM]
        kcat_ref[:, lo + QK_NOPE_DIM:lo + MLA_QK_PAD] = k_rope_odd if head % 2 else k_rope_even

    h = (unit * gq_ref[...]).astype(BF16)
    c_q = (_rms_unit(_dot(h, wqa_ref[...])) * gqa_ref[...]).astype(BF16)
    q_nope_t = lax.dot_general(wqt_ref[0:nope_all, :], c_q, _NT, preferred_element_type=F32)
    q_rope_t = lax.dot_general(wqt_ref[nope_all:, :], c_q, _NT, preferred_element_type=F32)
    cos_t = cost_ref[...]
    sin_t = sint_ref[...]
    zero = jnp.zeros((QK_ROPE_DIM, tm), BF16)
    for head in range(MLA_HEADS):
        lo = head * MLA_QK_PAD
        nope = q_nope_t[head * QK_NOPE_DIM:(head + 1) * QK_NOPE_DIM, :]
        rope = _rope_rows(q_rope_t[head * QK_ROPE_DIM:(head + 1) * QK_ROPE_DIM, :], cos_t, sin_t)
        rope = (rope * MLA_QSCALE).astype(BF16)
        qcat_t_ref[lo:lo + QK_NOPE_DIM, :] = (nope * MLA_QSCALE).astype(BF16)
        qcat_t_ref[lo + QK_NOPE_DIM:lo + MLA_QK_PAD, :] = jnp.concatenate(
            [zero, rope] if head % 2 else [rope, zero], axis=0)


def _mla_proj(x2d, g_kv, g_q, w_kv_a, g_kv_a, w_k_b, w_vt_b, w_q_a, g_q_a, w_qt_b, rope, seq):
    m = x2d.shape[0]
    tm = ROW_TILE
    row = lambda i: (i, 0)
    col = lambda i: (0, i)
    fixed = lambda i: (0, 0)
    pos = lambda i: (i % (seq // tm), 0)
    pos_t = lambda i: (0, i % (seq // tm))
    kva_cols = w_kv_a.shape[1]
    cat = MLA_HEADS * MLA_QK_PAD
    pipelined = tm * D_MODEL * 4 + 2 * tm * V7X_LANES * 4 + tm * (2 * cat + D_MODEL) * 2
    resident = (D_MODEL * kva_cols + KV_LORA_RANK * 2 * D_MODEL + D_MODEL * Q_LORA_RANK
                + Q_LORA_RANK * 3 * D_MODEL // 2) * 2 + 4 * D_MODEL * 4
    temps = tm * D_MODEL * (4 + 2 + 2) + 4 * tm * D_MODEL * 4
    return pl.pallas_call(
        _mla_proj_kernel,
        grid=(m // tm,),
        in_specs=[
            pl.BlockSpec((tm, D_MODEL), row),
            _resident((1, D_MODEL), fixed),
            _resident((1, D_MODEL), fixed),
            _resident((D_MODEL, kva_cols), fixed),
            _resident((1, KV_LORA_RANK), fixed),
            _resident((KV_LORA_RANK, MLA_HEADS * QK_NOPE_DIM), fixed),
            _resident((MLA_HEADS * V_HEAD_DIM, KV_LORA_RANK), fixed),
            _resident((D_MODEL, Q_LORA_RANK), fixed),
            _resident((1, Q_LORA_RANK), fixed),
            _resident((3 * D_MODEL // 2, Q_LORA_RANK), fixed),
            pl.BlockSpec((tm, V7X_LANES), pos),
            pl.BlockSpec((tm, V7X_LANES), pos),
            pl.BlockSpec((QK_ROPE_DIM, tm), pos_t),
            pl.BlockSpec((QK_ROPE_DIM, tm), pos_t),
        ],
        out_specs=[
            pl.BlockSpec((cat, tm), col),
            pl.BlockSpec((tm, cat), row),
            pl.BlockSpec((MLA_HEADS * V_HEAD_DIM, tm), col),
        ],
        out_shape=[
            jax.ShapeDtypeStruct((cat, m), BF16),
            jax.ShapeDtypeStruct((m, cat), BF16),
            jax.ShapeDtypeStruct((MLA_HEADS * V_HEAD_DIM, m), BF16),
        ],
        compiler_params=_compiler_params(1, pipelined, resident, temps),
        name="mla_proj",
    )(x2d, g_kv, g_q, w_kv_a, g_kv_a, w_k_b, w_vt_b, w_q_a, g_q_a, w_qt_b, *rope)


MLA_HEADS_PER_STEP = 2


def _mla_attn_kernel(qt_ref, qt_next_ref, k_ref, vt_ref, o_ref, s_scr, bmax_scr, m_scr, acc_scr):
    n = MLA_HEADS_PER_STEP
    tk = s_scr.shape[3]

    def scores(q_ref, j, h, strip, nk):
        feats = slice(h * MLA_QK_PAD, (h + 1) * MLA_QK_PAD)
        k = k_ref[0, pl.ds(pl.multiple_of(j * tk, tk), nk), feats]
        return _dot(k, q_ref[feats, strip * V7X_MXU_WIDTH:(strip + 1) * V7X_MXU_WIDTH])

    def value_fn(j, h, nk):
        rows = slice(h * V_HEAD_DIM, (h + 1) * V_HEAD_DIM)
        return vt_ref[rows, pl.ds(pl.multiple_of(j * tk, tk), nk)]

    _causal_attention(pl.program_id(2), n,
                      functools.partial(scores, qt_ref),
                      lambda h, strip: scores(qt_next_ref, 0, h, strip, tk),
                      value_fn, s_scr, bmax_scr, m_scr, acc_scr)
    o_ref[0] = _attention_output(acc_scr, n, V_HEAD_DIM).T.astype(o_ref.dtype)


def _mla_attn(q_cat_t, k_cat, v_t, b, s):
    t = ATTN_Q_TILE
    n_tiles = s // t
    n = MLA_HEADS_PER_STEP
    pipelined = n * (2 * t * MLA_QK_PAD * 2 + s * MLA_QK_PAD * 2 + s * V_HEAD_DIM * 2
                     + t * V_HEAD_DIM * 2)
    temps = (_attention_scratch_bytes(t, n, V_HEAD_DIM)
             + n * (t * t // 2 * (4 + 2) + t * V_HEAD_DIM * 4))
    return pl.pallas_call(
        _mla_attn_kernel,
        grid=(b, MLA_HEADS // n, n_tiles),
        in_specs=[
            pl.BlockSpec((n * MLA_QK_PAD, t), lambda i, h, j: (h, i * n_tiles + j)),
            pl.BlockSpec((n * MLA_QK_PAD, t),
                         lambda i, h, j: (h, i * n_tiles + jnp.minimum(j + 1, n_tiles - 1))),
            pl.BlockSpec((1, s, n * MLA_QK_PAD), lambda i, h, j: (i, 0, h)),
            pl.BlockSpec((n * V_HEAD_DIM, s), lambda i, h, j: (h, i)),
        ],
        out_specs=pl.BlockSpec((1, t, n * V_HEAD_DIM), lambda i, h, j: (i, j, h)),
        out_shape=jax.ShapeDtypeStruct((b, s, MLA_HEADS * V_HEAD_DIM), BF16),
        scratch_shapes=_attention_scratch(t, n, V_HEAD_DIM),
        compiler_params=_compiler_params(3, pipelined, 0, temps),
        name="mla_attn",
    )(q_cat_t, q_cat_t, k_cat, v_t)


def _rope_tables(seq):
    f32 = np.float32
    inv = (f32(1.0) / (f32(ROPE_BASE) ** (np.arange(0, QK_ROPE_DIM, 2, dtype=f32)
                                          / f32(QK_ROPE_DIM)))).astype(f32)
    ang = np.arange(seq, dtype=f32)[:, None] * inv[None, :]
    cos, sin = np.cos(ang), np.sin(ang)
    cos_head = np.concatenate([cos, cos], axis=1)
    sin_head = np.concatenate([-sin, sin], axis=1)
    reps = V7X_LANES // QK_ROPE_DIM
    tables = (np.tile(cos_head, (1, reps)), np.tile(sin_head, (1, reps)), cos_head.T, sin_head.T)
    return tuple(jnp.asarray(np.ascontiguousarray(t), dtype=F32) for t in tables)


def _row(vec):
    return vec.reshape(1, -1).astype(F32)


def kernel(x, norm_mix_g, norm_ffn_g, fox_w_in, fox_b_f, fox_w_out, kv_norm_g, mla_w_kv_a,
           mla_kv_a_norm_g, mla_w_kv_b, mla_w_q_a, mla_q_a_norm_g, mla_w_q_b, mla_w_out,
           ffn_w_up, ffn_w_down, final_norm_g):
    b, s, d = x.shape
    assert d == D_MODEL and s % ATTN_Q_TILE == 0 and s % ROW_TILE == 0
    assert fox_w_in.shape[0] == 1 and mla_w_q_a.shape[0] == 1
    x2d = x.reshape(b * s, d)

    b_f = jnp.pad(fox_b_f[0].astype(F32), (0, V7X_LANES - FOX_HEADS)).reshape(1, V7X_LANES)
    q_t, k, v_t, ext = _fox_inproj(x2d, _row(norm_mix_g[0]), fox_w_in[0].T, b_f, s)
    ext = ext.reshape(b, s, V7X_LANES)
    ctx = _fox_attn(q_t, k.reshape(b, s, FOX_WIDTH), ext, v_t)
    x2d = _mix_ffn(x2d, ctx.reshape(b * s, FOX_WIDTH), fox_w_out[0],
                   _row(norm_ffn_g[0]), ffn_w_up, ffn_w_down, 0, _row(final_norm_g),
                   final_norm=False)

    w_kv_a = jnp.pad(mla_w_kv_a, ((0, 0), (0, V7X_LANES - QK_ROPE_DIM))).astype(BF16)
    w_kv_b = mla_w_kv_b.reshape(KV_LORA_RANK, MLA_HEADS, QK_NOPE_DIM + V_HEAD_DIM)
    w_k_b = w_kv_b[:, :, :QK_NOPE_DIM].reshape(KV_LORA_RANK, -1).astype(BF16)
    w_vt_b = w_kv_b[:, :, QK_NOPE_DIM:].reshape(KV_LORA_RANK, -1).T.astype(BF16)
    w_q_b = mla_w_q_b[0].reshape(Q_LORA_RANK, MLA_HEADS, QK_NOPE_DIM + QK_ROPE_DIM)
    w_qt_b = jnp.concatenate(
        [w_q_b[:, :, :QK_NOPE_DIM].reshape(Q_LORA_RANK, -1),
         w_q_b[:, :, QK_NOPE_DIM:].reshape(Q_LORA_RANK, -1)], axis=1).T.astype(BF16)
    q_cat_t, k_cat, v_t = _mla_proj(
        x2d, _row(kv_norm_g), _row(norm_mix_g[1]), w_kv_a, _row(mla_kv_a_norm_g), w_k_b, w_vt_b,
        mla_w_q_a[0].astype(BF16), _row(mla_q_a_norm_g[0]), w_qt_b, _rope_tables(s), s)

    cat = MLA_HEADS * MLA_QK_PAD
    ctx = _mla_attn(q_cat_t, k_cat.reshape(b, s, cat), v_t, b, s)
    out = _mix_ffn(x2d, ctx.reshape(b * s, MLA_HEADS * V_HEAD_DIM), mla_w_out[0],
                   _row(norm_ffn_g[1]), ffn_w_up, ffn_w_down, 1, _row(final_norm_g),
                   final_norm=True)
    return out.reshape(b, s, d)
```

```python
import functools

import jax
import jax.numpy as jnp
import numpy as np
from jax import lax
from jax.experimental import pallas as pl
from jax.experimental.pallas import tpu as pltpu

D_MODEL = 1024
FOX_HEADS = 16
FOX_HEAD_DIM = 64
FOX_WIDTH = FOX_HEADS * FOX_HEAD_DIM
MLA_HEADS = 8
QK_NOPE_DIM = 128
QK_ROPE_DIM = 64
V_HEAD_DIM = 128
Q_LORA_RANK = 384
KV_LORA_RANK = 256
ROPE_BASE = 10000.0
D_FF = 4 * D_MODEL
EPS = 1e-6

V7X_LANES = 128
V7X_VMEM_BYTES = 64 * 1024 * 1024
BF16_SUBLANES = 16
V7X_MXU_WIDTH = 256

LOG2E = 1.4426950408889634
FOX_QSCALE = (FOX_HEAD_DIM ** -0.5) * LOG2E
MLA_QSCALE = ((QK_NOPE_DIM + QK_ROPE_DIM) ** -0.5) * LOG2E
MLA_QK_PAD = 256

ROW_TILE = 512
ATTN_Q_TILE = 1024
FF_CHUNK = 512

BF16 = jnp.bfloat16
F32 = jnp.float32
_NT = (((1,), (1,)), ((), ()))


def _compiler_params(n_grid, pipelined_bytes, resident_bytes, temp_bytes):
    need = 2 * pipelined_bytes + resident_bytes + temp_bytes
    return pltpu.CompilerParams(
        dimension_semantics=("arbitrary",) * n_grid,
        vmem_limit_bytes=min(need, V7X_VMEM_BYTES),
    )


def _resident(shape, index_map):
    return pl.BlockSpec(shape, index_map, pipeline_mode=pl.Buffered(1))


def _rms_unit(xf):
    return xf * lax.rsqrt(jnp.mean(xf * xf, axis=-1, keepdims=True) + EPS)


def _dot(a, b):
    return jnp.dot(a, b, preferred_element_type=F32)


def _split3(x):
    hi = x.astype(BF16)
    rem = x - hi.astype(F32)
    mid = rem.astype(BF16)
    lo = (rem - mid.astype(F32)).astype(BF16)
    return hi, mid, lo


def _log_sigmoid_pieces(z_t):
    ls = jnp.minimum(z_t, 0.0) - jnp.log1p(jnp.exp(-jnp.abs(z_t)))
    return jnp.concatenate(_split3(ls), axis=0)


def _decay_extension(pieces_t, carry_ref):
    heads = FOX_HEADS
    tokens = pieces_t.shape[1]
    row = lax.broadcasted_iota(jnp.int32, (tokens, tokens), 0)
    col = lax.broadcasted_iota(jnp.int32, (tokens, tokens), 1)
    upper = jnp.where(row <= col, 1.0, 0.0).astype(BF16)
    sums = _dot(pieces_t, upper)
    c_t = sums[0:heads] + sums[heads:2 * heads] + sums[2 * heads:3 * heads] + carry_ref[...]
    last = lax.broadcasted_iota(jnp.int32, c_t.shape, 1) == tokens - 1
    carry_ref[...] = jnp.sum(jnp.where(last, c_t, 0.0), axis=1, keepdims=True)
    stacked = jnp.concatenate(
        [piece.astype(F32) for piece in _split3(c_t * LOG2E)]
        + [jnp.zeros((V7X_LANES - 3 * heads, tokens), F32)], axis=0)
    return stacked.T.astype(BF16)


def _fox_inproj_kernel(x_ref, g_ref, w_ref, b_ref, qt_ref, k_ref, vt_ref, e_ref,
                       wk_scr, wgqv_scr, carry_scr, *, tiles_per_seq):
    heads = FOX_HEADS
    @pl.when(pl.program_id(0) % tiles_per_seq == 0)
    def _():
        carry_scr[...] = jnp.zeros_like(carry_scr)

    @pl.when(pl.program_id(0) == 0)
    def _():
        chunk = V7X_MXU_WIDTH
        wgqv_scr[0:heads, :] = w_ref[3 * FOX_WIDTH:3 * FOX_WIDTH + heads, :].astype(BF16)
        wgqv_scr[heads:heads + FOX_WIDTH, :] = w_ref[0:FOX_WIDTH, :].astype(BF16)
        wgqv_scr[heads + FOX_WIDTH:, :] = w_ref[2 * FOX_WIDTH:3 * FOX_WIDTH, :].astype(BF16)
        for c in range(0, FOX_WIDTH, chunk):
            block = w_ref[FOX_WIDTH + c:FOX_WIDTH + c + chunk, :]
            wk_scr[:, c:c + chunk] = block.T.astype(BF16)

    h = (_rms_unit(x_ref[...]) * g_ref[...]).astype(BF16)
    gq_t = lax.dot_general(wgqv_scr[0:heads + FOX_WIDTH, :], h, _NT, preferred_element_type=F32)
    gate_pieces = _log_sigmoid_pieces(gq_t[0:heads, :] + b_ref[...])
    qt_ref[...] = (gq_t[heads:, :] * FOX_QSCALE).astype(BF16)
    k_ref[...] = _dot(h, wk_scr[...]).astype(BF16)
    e_ref[...] = _decay_extension(gate_pieces, carry_scr)
    v_t = lax.dot_general(wgqv_scr[heads + FOX_WIDTH:, :], h, _NT, preferred_element_type=F32)
    vt_ref[...] = v_t.astype(BF16)


def _fox_inproj(x2d, g, w_in_t, b_f, seq):
    m = x2d.shape[0]
    tm = ROW_TILE
    in_cols = w_in_t.shape[0]
    row = lambda i: (i, 0)
    col = lambda i: (0, i)
    fixed = lambda i: (0, 0)
    feature_major = jax.ShapeDtypeStruct((FOX_WIDTH, m), BF16)
    gqv_rows = FOX_HEADS + 2 * FOX_WIDTH
    pipelined = tm * D_MODEL * 4 + 3 * tm * FOX_WIDTH * 2 + tm * V7X_LANES * 2
    resident = D_MODEL * 4 + D_MODEL * in_cols * 4 + FOX_HEADS * V7X_LANES * 4
    scratch = D_MODEL * (FOX_WIDTH + gqv_rows) * 2 + FOX_HEADS * V7X_LANES * 4
    temps = (scratch + tm * D_MODEL * (4 + 2) + 2 * tm * FOX_WIDTH * 4
             + 2 * tm * tm * 4 + 4 * tm * V7X_LANES * 4)
    return pl.pallas_call(
        functools.partial(_fox_inproj_kernel, tiles_per_seq=seq // tm),
        grid=(m // tm,),
        in_specs=[
            pl.BlockSpec((tm, D_MODEL), row),
            _resident((1, D_MODEL), fixed),
            _resident((in_cols, D_MODEL), fixed),
            _resident((FOX_HEADS, 1), fixed),
        ],
        out_specs=[
            pl.BlockSpec((FOX_WIDTH, tm), col),
            pl.BlockSpec((tm, FOX_WIDTH), row),
            pl.BlockSpec((FOX_WIDTH, tm), col),
            pl.BlockSpec((tm, V7X_LANES), row),
        ],
        out_shape=[feature_major, jax.ShapeDtypeStruct((m, FOX_WIDTH), BF16), feature_major,
                   jax.ShapeDtypeStruct((m, V7X_LANES), BF16)],
        scratch_shapes=[pltpu.VMEM((D_MODEL, FOX_WIDTH), BF16),
                        pltpu.VMEM((gqv_rows, D_MODEL), BF16),
                        pltpu.VMEM((FOX_HEADS, 1), F32)],
        compiler_params=_compiler_params(1, pipelined, resident, temps),
        name="fox_inproj",
    )(x2d, g, w_in_t, b_f)


def _causal_attention(qi, chains, score_fn, next_tile_score_fn, value_fn,
                      s_scr, bmax_scr, m_scr, acc_scr):
    _, _, strips, tk, strip_width = s_scr.shape
    assert strip_width == V7X_MXU_WIDTH and strips * strip_width == 2 * tk
    m_scr[...] = jnp.full(m_scr.shape, -jnp.inf, F32)
    acc_scr[...] = jnp.zeros(acc_scr.shape, F32)
    ones = jnp.ones((BF16_SUBLANES, tk), BF16)

    def lanes(n):
        return slice(n * V7X_MXU_WIDTH, (n + 1) * V7X_MXU_WIDTH)

    def store_scores(s_t, slot, c, n):
        s_scr[slot, c, n] = s_t
        bmax_scr[slot, c, :, lanes(n)] = jnp.max(s_t, axis=0, keepdims=True)

    def issue(j, slot, c, n):
        store_scores(score_fn(j, c, n, tk), slot, c, n)

    def visible_keys(n, key_offset):
        return min(tk, (n + 1) * V7X_MXU_WIDTH - key_offset)

    def issue_masked(j, slot, c, n, key_offset):
        nk = visible_keys(n, key_offset)
        s_scr[slot, c, n, 0:nk, :] = score_fn(j, c, n, nk)

    def consume(j, slot, c, n, key_offset=None):
        nk = tk if key_offset is None else visible_keys(n, key_offset)
        s_t = s_scr[slot, c, n, 0:nk, :]
        if key_offset is None:
            block_max = bmax_scr[slot, c, :, lanes(n)]
        else:
            key = lax.broadcasted_iota(jnp.int32, s_t.shape, 0) + key_offset
            query = lax.broadcasted_iota(jnp.int32, s_t.shape, 1) + n * V7X_MXU_WIDTH
            s_t = jnp.where(key <= query, s_t, -jnp.inf)
            block_max = jnp.max(s_t, axis=0, keepdims=True)
        m = m_scr[c, :, lanes(n)]
        m_new = jnp.maximum(m, block_max)
        p_t = jnp.exp2(s_t - m_new).astype(BF16)
        m_scr[c, :, lanes(n)] = m_new
        v_ones = jnp.concatenate([value_fn(j, c, nk), ones[:, 0:nk]], axis=0)
        acc_scr[c, :, lanes(n)] = (jnp.exp2(m - m_new) * acc_scr[c, :, lanes(n)]
                                   + _dot(v_ones, p_t))

    units = [(c, n) for n in range(strips) for c in range(chains)]

    def consume_and_issue(j, slot):
        for c, n in units:
            consume(j, slot, c, n)
            issue(j + 1, 1 - slot, c, n)

    @pl.when(qi == 0)
    def _():
        for c, n in units:
            issue(0, 0, c, n)

    def two_blocks(i):
        consume_and_issue(2 * i, 0)
        consume_and_issue(2 * i + 1, 1)

    def four_blocks(i, carry):
        two_blocks(2 * i)
        two_blocks(2 * i + 1)
        return carry

    lax.fori_loop(0, qi // 2, four_blocks, 0)

    @pl.when(qi % 2 == 1)
    def _():
        two_blocks(qi - 1)

    half = strips // 2
    for c, n in units:
        consume(2 * qi, 0, c, n, key_offset=0 if n < half else None)
        if n >= half:
            issue_masked(2 * qi + 1, 1, c, n, tk)
        else:
            store_scores(next_tile_score_fn(c, n), 0, c, n)
    for c, n in units:
        if n >= half:
            consume(2 * qi + 1, 1, c, n, key_offset=tk)
            store_scores(next_tile_score_fn(c, n), 0, c, n)


def _attention_scratch(tq, chains, width):
    tk = tq // 2
    return [pltpu.VMEM((2, chains, tq // V7X_MXU_WIDTH, tk, V7X_MXU_WIDTH), F32),
            pltpu.VMEM((2, chains, 1, tq), F32),
            pltpu.VMEM((chains, 1, tq), F32),
            pltpu.VMEM((chains, width + BF16_SUBLANES, tq), F32)]


def _attention_scratch_bytes(tq, chains, width):
    return chains * tq * tq * 4 + chains * (3 * 8 + width + BF16_SUBLANES) * tq * 4


def _attention_output(acc_scr, chains, width):
    return jnp.concatenate(
        [acc_scr[c, 0:width, :] / acc_scr[c, width:width + 1, :] for c in range(chains)], axis=0)


FOX_HEADS_PER_STEP = 4


def _fox_attn_kernel(qt_ref, qt_next_ref, k_ref, e_ref, vt_ref, o_ref,
                     qa_scr, s_scr, bmax_scr, m_scr, acc_scr):
    n_heads = FOX_HEADS_PER_STEP
    tk = s_scr.shape[3]
    first_head = n_heads * pl.program_id(1)
    qi = pl.program_id(2)
    cur = qi % 2

    def pair_lanes(h):
        return slice((h // 2) * V7X_LANES, (h // 2 + 1) * V7X_LANES)

    def put_features(slot, ref):
        row = lax.broadcasted_iota(jnp.int32, (V7X_LANES, ref.shape[1]), 0)
        for h in range(n_heads):
            q_t = ref[pair_lanes(h), :]
            own = (row >= (h % 2) * FOX_HEAD_DIM) & (row < (h % 2 + 1) * FOX_HEAD_DIM)
            qa_scr[slot, h, 0:V7X_LANES, :] = jnp.where(own, q_t, jnp.zeros_like(q_t))

    @pl.when(qi == 0)
    def _():
        put_features(0, qt_ref)
        row = lax.broadcasted_iota(jnp.int32, (V7X_LANES, qt_ref.shape[1]), 0)
        for h in range(n_heads):
            head = first_head + h
            bias_row = ((row == head) | (row == head + FOX_HEADS) | (row == head + 2 * FOX_HEADS))
            selector = jnp.where(bias_row, -1.0, 0.0).astype(BF16)
            qa_scr[0, h, V7X_LANES:, :] = selector
            qa_scr[1, h, V7X_LANES:, :] = selector

    put_features(1 - cur, qt_next_ref)

    def keys(j, h, nk):
        ks = pl.multiple_of(j * tk, tk)
        return jnp.concatenate([k_ref[0, pl.ds(ks, nk), pair_lanes(h)],
                                e_ref[0, pl.ds(ks, nk), :]], axis=1)

    def strip(n):
        return slice(n * V7X_MXU_WIDTH, (n + 1) * V7X_MXU_WIDTH)

    def score_fn(j, h, n, nk):
        return _dot(keys(j, h, nk), qa_scr[cur, h, :, strip(n)])

    def next_tile_score_fn(h, n):
        return _dot(keys(0, h, tk), qa_scr[1 - cur, h, :, strip(n)])

    def value_fn(j, h, nk):
        rows = slice(h * FOX_HEAD_DIM, (h + 1) * FOX_HEAD_DIM)
        return vt_ref[rows, pl.ds(pl.multiple_of(j * tk, tk), nk)]

    _causal_attention(qi, n_heads, score_fn, next_tile_score_fn,
                      value_fn, s_scr, bmax_scr, m_scr, acc_scr)
    o_ref[0] = _attention_output(acc_scr, n_heads, FOX_HEAD_DIM).T.astype(o_ref.dtype)


def _fox_attn(q_t, k, ext, v_t):
    b, s, _ = k.shape
    t = ATTN_Q_TILE
    n_tiles = s // t
    n_heads = FOX_HEADS_PER_STEP
    width = n_heads * FOX_HEAD_DIM
    pipelined = 3 * t * width * 2 + 2 * s * width * 2 + s * V7X_LANES * 2
    scratch = (2 * n_heads * 2 * V7X_LANES * t * 2
               + _attention_scratch_bytes(t, n_heads, FOX_HEAD_DIM))
    temps = scratch + n_heads * (t * t // 2 * (4 + 2) + t * V7X_LANES * 4)
    return pl.pallas_call(
        _fox_attn_kernel,
        grid=(b, FOX_HEADS // n_heads, n_tiles),
        in_specs=[
            pl.BlockSpec((width, t), lambda i, p, j: (p, i * n_tiles + j)),
            pl.BlockSpec((width, t),
                         lambda i, p, j: (p, i * n_tiles + jnp.minimum(j + 1, n_tiles - 1))),
            pl.BlockSpec((1, s, width), lambda i, p, j: (i, 0, p)),
            pl.BlockSpec((1, s, V7X_LANES), lambda i, p, j: (i, 0, 0)),
            pl.BlockSpec((width, s), lambda i, p, j: (p, i)),
        ],
        out_specs=pl.BlockSpec((1, t, width), lambda i, p, j: (i, j, p)),
        out_shape=jax.ShapeDtypeStruct((b, s, FOX_WIDTH), BF16),
        scratch_shapes=[pltpu.VMEM((2, n_heads, 2 * V7X_LANES, t), BF16)]
        + _attention_scratch(t, n_heads, FOX_HEAD_DIM),
        compiler_params=_compiler_params(3, pipelined, 0, temps),
        name="fox_attn",
    )(q_t, q_t, k, ext, v_t)


def _mix_ffn_kernel(x_ref, ctx_ref, wo_ref, g_ref, wu_ref, wd_ref, gf_ref, o_ref, *, final_norm):
    x1 = x_ref[...] + _dot(ctx_ref[...], wo_ref[...].astype(BF16))
    hn = (_rms_unit(x1) * g_ref[...]).astype(BF16)
    acc = x1
    for c in range(D_FF // FF_CHUNK):
        cols = slice(c * FF_CHUNK, (c + 1) * FF_CHUNK)
        up = jnp.maximum(_dot(hn, wu_ref[:, cols].astype(BF16)), 0.0)
        acc = acc + _dot((up * up).astype(BF16), wd_ref[cols, :].astype(BF16))
    if final_norm:
        acc = _rms_unit(acc) * gf_ref[...]
    o_ref[...] = acc


def _mix_ffn(x2d, ctx2d, w_out, g_ffn, w_up_all, w_down_all, layer, g_final, final_norm):
    m = x2d.shape[0]
    tm = ROW_TILE
    row = lambda i: (i, 0)
    fixed = lambda i: (0, 0)
    of_layer = lambda i: (layer, 0, 0)
    pipelined = tm * D_MODEL * (4 + 2 + 4)
    resident = (D_MODEL * D_MODEL + 2 * D_MODEL * D_FF) * 4 + 2 * D_MODEL * 4
    temps = (tm * D_MODEL * (4 + 4 + 2) + tm * FF_CHUNK * (4 + 4 + 2)
             + 2 * D_MODEL * FF_CHUNK * 2 + D_MODEL * D_MODEL * 2)
    return pl.pallas_call(
        functools.partial(_mix_ffn_kernel, final_norm=final_norm),
        grid=(m // tm,),
        in_specs=[
            pl.BlockSpec((tm, D_MODEL), row),
            pl.BlockSpec((tm, D_MODEL), row),
            _resident((D_MODEL, D_MODEL), fixed),
            _resident((1, D_MODEL), fixed),
            _resident((None, D_MODEL, D_FF), of_layer),
            _resident((None, D_FF, D_MODEL), of_layer),
            _resident((1, D_MODEL), fixed),
        ],
        out_specs=pl.BlockSpec((tm, D_MODEL), row),
        out_shape=jax.ShapeDtypeStruct((m, D_MODEL), F32),
        compiler_params=_compiler_params(1, pipelined, resident, temps),
        name="mix_ffn_final" if final_norm else "mix_ffn",
    )(x2d, ctx2d, w_out, g_ffn, w_up_all, w_down_all, g_final)


def _rope_lanes(t, cos, sin_signed):
    lane = lax.broadcasted_iota(jnp.int32, t.shape, 1)
    low = (lane % QK_ROPE_DIM) < (QK_ROPE_DIM // 2)
    partner = jnp.where(low,
                        pltpu.roll(t, V7X_LANES - QK_ROPE_DIM // 2, 1),
                        pltpu.roll(t, QK_ROPE_DIM // 2, 1))
    return t * cos + partner * sin_signed


def _rope_rows(t, cos_t, sin_signed_t):
    half = QK_ROPE_DIM // 2
    partner = jnp.concatenate([t[half:, :], t[:half, :]], axis=0)
    return t * cos_t + partner * sin_signed_t


def _mla_proj_kernel(x_ref, gkv_ref, gq_ref, wkva_ref, gkva_ref, wkb_ref, wvt_ref, wqa_ref,
                     gqa_ref, wqt_ref, cos_ref, sin_ref, cost_ref, sint_ref,
                     qcat_t_ref, kcat_ref, vt_ref):
    unit = _rms_unit(x_ref[...])
    tm = unit.shape[0]
    nope_all = MLA_HEADS * QK_NOPE_DIM

    src = (unit * gkv_ref[...]).astype(BF16)
    kv_a = _dot(src, wkva_ref[...])
    c_kv = (_rms_unit(kv_a[:, 0:KV_LORA_RANK]) * gkva_ref[...]).astype(BF16)
    k_rope = _rope_lanes(kv_a[:, KV_LORA_RANK:KV_LORA_RANK + V7X_LANES],
                         cos_ref[...], sin_ref[...])
    k_rope_even = k_rope.astype(BF16)
    k_rope_odd = pltpu.roll(k_rope, QK_ROPE_DIM, 1).astype(BF16)
    k_nope = _dot(c_kv, wkb_ref[...]).astype(BF16)
    vt_ref[...] = lax.dot_general(wvt_ref[...], c_kv, _NT, preferred_element_type=F32).astype(BF16)
    for head in range(MLA_HEADS):
        lo = head * MLA_QK_PAD
        kcat_ref[:, lo:lo + QK_NOPE_DIM] = k_nope[:, head * QK_NOPE_DIM:(head + 1) * QK_NOPE_DIM]
        kcat_ref[:, lo + QK_NOPE_DIM:lo + MLA_QK_PAD] = k_rope_odd if head % 2 else k_rope_even

    h = (unit * gq_ref[...]).astype(BF16)
    c_q = (_rms_unit(_dot(h, wqa_ref[...])) * gqa_ref[...]).astype(BF16)
    q_nope_t = lax.dot_general(wqt_ref[0:nope_all, :], c_q, _NT, preferred_element_type=F32)
    q_rope_t = lax.dot_general(wqt_ref[nope_all:, :], c_q, _NT, preferred_element_type=F32)
    cos_t = cost_ref[...]
    sin_t = sint_ref[...]
    zero = jnp.zeros((QK_ROPE_DIM, tm), BF16)
    for head in range(MLA_HEADS):
        lo = head * MLA_QK_PAD
        nope = q_nope_t[head * QK_NOPE_DIM:(head + 1) * QK_NOPE_DIM, :]
        rope = _rope_rows(q_rope_t[head * QK_ROPE_DIM:(head + 1) * QK_ROPE_DIM, :], cos_t, sin_t)
        rope = (rope * MLA_QSCALE).astype(BF16)
        qcat_t_ref[lo:lo + QK_NOPE_DIM, :] = (nope * MLA_QSCALE).astype(BF16)
        qcat_t_ref[lo + QK_NOPE_DIM:lo + MLA_QK_PAD, :] = jnp.concatenate(
            [zero, rope] if head % 2 else [rope, zero], axis=0)


def _mla_proj(x2d, g_kv, g_q, w_kv_a, g_kv_a, w_k_b, w_vt_b, w_q_a, g_q_a, w_qt_b, rope, seq):
    m = x2d.shape[0]
    tm = ROW_TILE
    row = lambda i: (i, 0)
    col = lambda i: (0, i)
    fixed = lambda i: (0, 0)
    pos = lambda i: (i % (seq // tm), 0)
    pos_t = lambda i: (0, i % (seq // tm))
    kva_cols = w_kv_a.shape[1]
    cat = MLA_HEADS * MLA_QK_PAD
    pipelined = tm * D_MODEL * 4 + 2 * tm * V7X_LANES * 4 + tm * (2 * cat + D_MODEL) * 2
    resident = (D_MODEL * kva_cols + KV_LORA_RANK * 2 * D_MODEL + D_MODEL * Q_LORA_RANK
                + Q_LORA_RANK * 3 * D_MODEL // 2) * 2 + 4 * D_MODEL * 4
    temps = tm * D_MODEL * (4 + 2 + 2) + 4 * tm * D_MODEL * 4
    return pl.pallas_call(
        _mla_proj_kernel,
        grid=(m // tm,),
        in_specs=[
            pl.BlockSpec((tm, D_MODEL), row),
            _resident((1, D_MODEL), fixed),
            _resident((1, D_MODEL), fixed),
            _resident((D_MODEL, kva_cols), fixed),
            _resident((1, KV_LORA_RANK), fixed),
            _resident((KV_LORA_RANK, MLA_HEADS * QK_NOPE_DIM), fixed),
            _resident((MLA_HEADS * V_HEAD_DIM, KV_LORA_RANK), fixed),
            _resident((D_MODEL, Q_LORA_RANK), fixed),
            _resident((1, Q_LORA_RANK), fixed),
            _resident((3 * D_MODEL // 2, Q_LORA_RANK), fixed),
            pl.BlockSpec((tm, V7X_LANES), pos),
            pl.BlockSpec((tm, V7X_LANES), pos),
            pl.BlockSpec((QK_ROPE_DIM, tm), pos_t),
            pl.BlockSpec((QK_ROPE_DIM, tm), pos_t),
        ],
        out_specs=[
            pl.BlockSpec((cat, tm), col),
            pl.BlockSpec((tm, cat), row),
            pl.BlockSpec((MLA_HEADS * V_HEAD_DIM, tm), col),
        ],
        out_shape=[
            jax.ShapeDtypeStruct((cat, m), BF16),
            jax.ShapeDtypeStruct((m, cat), BF16),
            jax.ShapeDtypeStruct((MLA_HEADS * V_HEAD_DIM, m), BF16),
        ],
        compiler_params=_compiler_params(1, pipelined, resident, temps),
        name="mla_proj",
    )(x2d, g_kv, g_q, w_kv_a, g_kv_a, w_k_b, w_vt_b, w_q_a, g_q_a, w_qt_b, *rope)


MLA_HEADS_PER_STEP = 2


def _mla_attn_kernel(qt_ref, qt_next_ref, k_ref, vt_ref, o_ref, s_scr, bmax_scr, m_scr, acc_scr):
    n = MLA_HEADS_PER_STEP
    tk = s_scr.shape[3]

    def scores(q_ref, j, h, strip, nk):
        feats = slice(h * MLA_QK_PAD, (h + 1) * MLA_QK_PAD)
        k = k_ref[0, pl.ds(pl.multiple_of(j * tk, tk), nk), feats]
        return _dot(k, q_ref[feats, strip * V7X_MXU_WIDTH:(strip + 1) * V7X_MXU_WIDTH])

    def value_fn(j, h, nk):
        rows = slice(h * V_HEAD_DIM, (h + 1) * V_HEAD_DIM)
        return vt_ref[rows, pl.ds(pl.multiple_of(j * tk, tk), nk)]

    _causal_attention(pl.program_id(2), n,
                      functools.partial(scores, qt_ref),
                      lambda h, strip: scores(qt_next_ref, 0, h, strip, tk),
                      value_fn, s_scr, bmax_scr, m_scr, acc_scr)
    o_ref[0] = _attention_output(acc_scr, n, V_HEAD_DIM).T.astype(o_ref.dtype)


def _mla_attn(q_cat_t, k_cat, v_t, b, s):
    t = ATTN_Q_TILE
    n_tiles = s // t
    n = MLA_HEADS_PER_STEP
    pipelined = n * (2 * t * MLA_QK_PAD * 2 + s * MLA_QK_PAD * 2 + s * V_HEAD_DIM * 2
                     + t * V_HEAD_DIM * 2)
    temps = (_attention_scratch_bytes(t, n, V_HEAD_DIM)
             + n * (t * t // 2 * (4 + 2) + t * V_HEAD_DIM * 4))
    return pl.pallas_call(
        _mla_attn_kernel,
        grid=(b, MLA_HEADS // n, n_tiles),
        in_specs=[
            pl.BlockSpec((n * MLA_QK_PAD, t), lambda i, h, j: (h, i * n_tiles + j)),
            pl.BlockSpec((n * MLA_QK_PAD, t),
                         lambda i, h, j: (h, i * n_tiles + jnp.minimum(j + 1, n_tiles - 1))),
            pl.BlockSpec((1, s, n * MLA_QK_PAD), lambda i, h, j: (i, 0, h)),
            pl.BlockSpec((n * V_HEAD_DIM, s), lambda i, h, j: (h, i)),
        ],
        out_specs=pl.BlockSpec((1, t, n * V_HEAD_DIM), lambda i, h, j: (i, j, h)),
        out_shape=jax.ShapeDtypeStruct((b, s, MLA_HEADS * V_HEAD_DIM), BF16),
        scratch_shapes=_attention_scratch(t, n, V_HEAD_DIM),
        compiler_params=_compiler_params(3, pipelined, 0, temps),
        name="mla_attn",
    )(q_cat_t, q_cat_t, k_cat, v_t)


def _rope_tables(seq):
    f32 = np.float32
    inv = (f32(1.0) / (f32(ROPE_BASE) ** (np.arange(0, QK_ROPE_DIM, 2, dtype=f32)
                                          / f32(QK_ROPE_DIM)))).astype(f32)
    ang = np.arange(seq, dtype=f32)[:, None] * inv[None, :]
    cos, sin = np.cos(ang), np.sin(ang)
    cos_head = np.concatenate([cos, cos], axis=1)
    sin_head = np.concatenate([-sin, sin], axis=1)
    reps = V7X_LANES // QK_ROPE_DIM
    tables = (np.tile(cos_head, (1, reps)), np.tile(sin_head, (1, reps)), cos_head.T, sin_head.T)
    return tuple(jnp.asarray(np.ascontiguousarray(t), dtype=F32) for t in tables)


def _row(vec):
    return vec.reshape(1, -1).astype(F32)


def kernel(x, norm_mix_g, norm_ffn_g, fox_w_in, fox_b_f, fox_w_out, kv_norm_g, mla_w_kv_a,
           mla_kv_a_norm_g, mla_w_kv_b, mla_w_q_a, mla_q_a_norm_g, mla_w_q_b, mla_w_out,
           ffn_w_up, ffn_w_down, final_norm_g):
    b, s, d = x.shape
    assert d == D_MODEL and s % ATTN_Q_TILE == 0 and s % ROW_TILE == 0
    assert fox_w_in.shape[0] == 1 and mla_w_q_a.shape[0] == 1
    x2d = x.reshape(b * s, d)

    b_f = fox_b_f[0].astype(F32).reshape(FOX_HEADS, 1)
    q_t, k, v_t, ext = _fox_inproj(x2d, _row(norm_mix_g[0]), fox_w_in[0].T, b_f, s)
    ext = ext.reshape(b, s, V7X_LANES)
    ctx = _fox_attn(q_t, k.reshape(b, s, FOX_WIDTH), ext, v_t)
    x2d = _mix_ffn(x2d, ctx.reshape(b * s, FOX_WIDTH), fox_w_out[0],
                   _row(norm_ffn_g[0]), ffn_w_up, ffn_w_down, 0, _row(final_norm_g),
                   final_norm=False)

    w_kv_a = jnp.pad(mla_w_kv_a, ((0, 0), (0, V7X_LANES - QK_ROPE_DIM))).astype(BF16)
    w_kv_b = mla_w_kv_b.reshape(KV_LORA_RANK, MLA_HEADS, QK_NOPE_DIM + V_HEAD_DIM)
    w_k_b = w_kv_b[:, :, :QK_NOPE_DIM].reshape(KV_LORA_RANK, -1).astype(BF16)
    w_vt_b = w_kv_b[:, :, QK_NOPE_DIM:].reshape(KV_LORA_RANK, -1).T.astype(BF16)
    w_q_b = mla_w_q_b[0].reshape(Q_LORA_RANK, MLA_HEADS, QK_NOPE_DIM + QK_ROPE_DIM)
    w_qt_b = jnp.concatenate(
        [w_q_b[:, :, :QK_NOPE_DIM].reshape(Q_LORA_RANK, -1),
         w_q_b[:, :, QK_NOPE_DIM:].reshape(Q_LORA_RANK, -1)], axis=1).T.astype(BF16)
    q_cat_t, k_cat, v_t = _mla_proj(
        x2d, _row(kv_norm_g), _row(norm_mix_g[1]), w_kv_a, _row(mla_kv_a_norm_g), w_k_b, w_vt_b,
        mla_w_q_a[0].astype(BF16), _row(mla_q_a_norm_g[0]), w_qt_b, _rope_tables(s), s)

    cat = MLA_HEADS * MLA_QK_PAD
    ctx = _mla_attn(q_cat_t, k_cat.reshape(b, s, cat), v_t, b, s)
    out = _mix_ffn(x2d, ctx.reshape(b * s, MLA_HEADS * V_HEAD_DIM), mla_w_out[0],
                   _row(norm_ffn_g[1]), ffn_w_up, ffn_w_down, 1, _row(final_norm_g),
                   final_norm=True)
    return out.reshape(b, s, d)
```

```python
import functools

import jax
import jax.numpy as jnp
import numpy as np
from jax import lax
from jax.experimental import pallas as pl
from jax.experimental.pallas import tpu as pltpu

D_MODEL = 1024
FOX_HEADS = 16
FOX_HEAD_DIM = 64
FOX_WIDTH = FOX_HEADS * FOX_HEAD_DIM
MLA_HEADS = 8
QK_NOPE_DIM = 128
QK_ROPE_DIM = 64
V_HEAD_DIM = 128
Q_LORA_RANK = 384
KV_LORA_RANK = 256
ROPE_BASE = 10000.0
D_FF = 4 * D_MODEL
EPS = 1e-6

V7X_LANES = 128
V7X_VMEM_BYTES = 64 * 1024 * 1024
BF16_SUBLANES = 16
V7X_MXU_WIDTH = 256

LOG2E = 1.4426950408889634
FOX_QSCALE = (FOX_HEAD_DIM ** -0.5) * LOG2E
MLA_QSCALE = ((QK_NOPE_DIM + QK_ROPE_DIM) ** -0.5) * LOG2E
MLA_QK_PAD = 256

ROW_TILE = 512
ATTN_Q_TILE = 1024
FF_CHUNK = 512

BF16 = jnp.bfloat16
F32 = jnp.float32
_NT = (((1,), (1,)), ((), ()))


def _compiler_params(n_grid, pipelined_bytes, resident_bytes, temp_bytes):
    need = 2 * pipelined_bytes + resident_bytes + temp_bytes
    return pltpu.CompilerParams(
        dimension_semantics=("arbitrary",) * n_grid,
        vmem_limit_bytes=min(need, V7X_VMEM_BYTES),
    )


def _resident(shape, index_map):
    return pl.BlockSpec(shape, index_map, pipeline_mode=pl.Buffered(1))


def _rms_unit(xf):
    return xf * lax.rsqrt(jnp.mean(xf * xf, axis=-1, keepdims=True) + EPS)


def _dot(a, b):
    return jnp.dot(a, b, preferred_element_type=F32)


def _split3(x):
    hi = x.astype(BF16)
    rem = x - hi.astype(F32)
    mid = rem.astype(BF16)
    lo = (rem - mid.astype(F32)).astype(BF16)
    return hi, mid, lo


def _log_sigmoid_pieces(z_t):
    ls = jnp.minimum(z_t, 0.0) - jnp.log1p(jnp.exp(-jnp.abs(z_t)))
    return jnp.concatenate(_split3(ls), axis=0)


def _decay_extension(pieces_t, carry_ref):
    heads = FOX_HEADS
    tokens = pieces_t.shape[1]
    row = lax.broadcasted_iota(jnp.int32, (tokens, tokens), 0)
    col = lax.broadcasted_iota(jnp.int32, (tokens, tokens), 1)
    upper = jnp.where(row <= col, 1.0, 0.0).astype(BF16)
    sums = _dot(pieces_t, upper)
    c_t = sums[0:heads] + sums[heads:2 * heads] + sums[2 * heads:3 * heads] + carry_ref[...]
    last = lax.broadcasted_iota(jnp.int32, c_t.shape, 1) == tokens - 1
    carry_ref[...] = jnp.sum(jnp.where(last, c_t, 0.0), axis=1, keepdims=True)
    stacked = jnp.concatenate(
        [piece.astype(F32) for piece in _split3(c_t * LOG2E)]
        + [jnp.zeros((V7X_LANES - 3 * heads, tokens), F32)], axis=0)
    return stacked.T.astype(BF16)


def _fox_inproj_kernel(x_ref, g_ref, w_ref, b_ref, qt_ref, k_ref, vt_ref, e_ref,
                       wk_scr, wgqv_scr, carry_scr, *, tiles_per_seq):
    heads = FOX_HEADS
    @pl.when(pl.program_id(0) % tiles_per_seq == 0)
    def _():
        carry_scr[...] = jnp.zeros_like(carry_scr)

    @pl.when(pl.program_id(0) == 0)
    def _():
        chunk = V7X_MXU_WIDTH
        wgqv_scr[0:heads, :] = w_ref[3 * FOX_WIDTH:3 * FOX_WIDTH + heads, :].astype(BF16)
        wgqv_scr[heads:heads + FOX_WIDTH, :] = w_ref[0:FOX_WIDTH, :].astype(BF16)
        wgqv_scr[heads + FOX_WIDTH:, :] = w_ref[2 * FOX_WIDTH:3 * FOX_WIDTH, :].astype(BF16)
        for c in range(0, FOX_WIDTH, chunk):
            block = w_ref[FOX_WIDTH + c:FOX_WIDTH + c + chunk, :]
            wk_scr[:, c:c + chunk] = block.T.astype(BF16)

    h = (_rms_unit(x_ref[...]) * g_ref[...]).astype(BF16)
    gq_t = lax.dot_general(wgqv_scr[0:heads + FOX_WIDTH, :], h, _NT, preferred_element_type=F32)
    gate_pieces = _log_sigmoid_pieces(gq_t[0:heads, :] + b_ref[...])
    qt_ref[...] = (gq_t[heads:, :] * FOX_QSCALE).astype(BF16)
    k_ref[...] = _dot(h, wk_scr[...]).astype(BF16)
    e_ref[...] = _decay_extension(gate_pieces, carry_scr)
    v_t = lax.dot_general(wgqv_scr[heads + FOX_WIDTH:, :], h, _NT, preferred_element_type=F32)
    vt_ref[...] = v_t.astype(BF16)


def _fox_inproj(x2d, g, w_in_t, b_f, seq):
    m = x2d.shape[0]
    tm = ROW_TILE
    in_cols = w_in_t.shape[0]
    row = lambda i: (i, 0)
    col = lambda i: (0, i)
    fixed = lambda i: (0, 0)
    feature_major = jax.ShapeDtypeStruct((FOX_WIDTH, m), BF16)
    gqv_rows = FOX_HEADS + 2 * FOX_WIDTH
    pipelined = tm * D_MODEL * 4 + 3 * tm * FOX_WIDTH * 2 + tm * V7X_LANES * 2
    resident = D_MODEL * 4 + D_MODEL * in_cols * 4 + FOX_HEADS * V7X_LANES * 4
    scratch = D_MODEL * (FOX_WIDTH + gqv_rows) * 2 + FOX_HEADS * V7X_LANES * 4
    temps = (scratch + tm * D_MODEL * (4 + 2) + 2 * tm * FOX_WIDTH * 4
             + 2 * tm * tm * 4 + 4 * tm * V7X_LANES * 4)
    return pl.pallas_call(
        functools.partial(_fox_inproj_kernel, tiles_per_seq=seq // tm),
        grid=(m // tm,),
        in_specs=[
            pl.BlockSpec((tm, D_MODEL), row),
            _resident((1, D_MODEL), fixed),
            _resident((in_cols, D_MODEL), fixed),
            _resident((FOX_HEADS, 1), fixed),
        ],
        out_specs=[
            pl.BlockSpec((FOX_WIDTH, tm), col),
            pl.BlockSpec((tm, FOX_WIDTH), row),
            pl.BlockSpec((FOX_WIDTH, tm), col),
            pl.BlockSpec((tm, V7X_LANES), row),
        ],
        out_shape=[feature_major, jax.ShapeDtypeStruct((m, FOX_WIDTH), BF16), feature_major,
                   jax.ShapeDtypeStruct((m, V7X_LANES), BF16)],
        scratch_shapes=[pltpu.VMEM((D_MODEL, FOX_WIDTH), BF16),
                        pltpu.VMEM((gqv_rows, D_MODEL), BF16),
                        pltpu.VMEM((FOX_HEADS, 1), F32)],
        compiler_params=_compiler_params(1, pipelined, resident, temps),
        name="fox_inproj",
    )(x2d, g, w_in_t, b_f)


def _causal_attention(qi, chains, score_fn, next_tile_score_fn, value_fn,
                      s_scr, bmax_scr, m_scr, acc_scr):
    _, _, strips, tk, strip_width = s_scr.shape
    assert strip_width == V7X_MXU_WIDTH and strips * strip_width == 2 * tk
    m_scr[...] = jnp.full(m_scr.shape, -jnp.inf, F32)
    acc_scr[...] = jnp.zeros(acc_scr.shape, F32)
    ones = jnp.ones((BF16_SUBLANES, tk), BF16)

    def lanes(n):
        return slice(n * V7X_MXU_WIDTH, (n + 1) * V7X_MXU_WIDTH)

    def store_scores(s_t, slot, c, n):
        s_scr[slot, c, n] = s_t
        bmax_scr[slot, c, :, lanes(n)] = jnp.max(s_t, axis=0, keepdims=True)

    def issue(j, slot, c, n):
        store_scores(score_fn(j, c, n, tk), slot, c, n)

    def visible_keys(n, key_offset):
        return min(tk, (n + 1) * V7X_MXU_WIDTH - key_offset)

    def issue_masked(j, slot, c, n, key_offset):
        nk = visible_keys(n, key_offset)
        s_scr[slot, c, n, 0:nk, :] = score_fn(j, c, n, nk)

    def consume(j, slot, c, n, key_offset=None):
        nk = tk if key_offset is None else visible_keys(n, key_offset)
        s_t = s_scr[slot, c, n, 0:nk, :]
        if key_offset is None:
            block_max = bmax_scr[slot, c, :, lanes(n)]
        else:
            key = lax.broadcasted_iota(jnp.int32, s_t.shape, 0) + key_offset
            query = lax.broadcasted_iota(jnp.int32, s_t.shape, 1) + n * V7X_MXU_WIDTH
            s_t = jnp.where(key <= query, s_t, -jnp.inf)
            block_max = jnp.max(s_t, axis=0, keepdims=True)
        m = m_scr[c, :, lanes(n)]
        m_new = jnp.maximum(m, block_max)
        p_t = jnp.exp2(s_t - m_new).astype(BF16)
        m_scr[c, :, lanes(n)] = m_new
        v_ones = jnp.concatenate([value_fn(j, c, nk), ones[:, 0:nk]], axis=0)
        acc_scr[c, :, lanes(n)] = (jnp.exp2(m - m_new) * acc_scr[c, :, lanes(n)]
                                   + _dot(v_ones, p_t))

    units = [(c, n) for n in range(strips) for c in range(chains)]

    def consume_and_issue(j, slot):
        for c, n in units:
            consume(j, slot, c, n)
            issue(j + 1, 1 - slot, c, n)

    @pl.when(qi == 0)
    def _():
        for c, n in units:
            issue(0, 0, c, n)

    def two_blocks(i):
        consume_and_issue(2 * i, 0)
        consume_and_issue(2 * i + 1, 1)

    def four_blocks(i, carry):
        two_blocks(2 * i)
        two_blocks(2 * i + 1)
        return carry

    lax.fori_loop(0, qi // 2, four_blocks, 0)

    @pl.when(qi % 2 == 1)
    def _():
        two_blocks(qi - 1)

    half = strips // 2
    for c, n in units:
        consume(2 * qi, 0, c, n, key_offset=0 if n < half else None)
        if n >= half:
            issue_masked(2 * qi + 1, 1, c, n, tk)
        else:
            store_scores(next_tile_score_fn(c, n), 0, c, n)
    for c, n in units:
        if n >= half:
            consume(2 * qi + 1, 1, c, n, key_offset=tk)
            store_scores(next_tile_score_fn(c, n), 0, c, n)


def _attention_scratch(tq, chains, width):
    tk = tq // 2
    return [pltpu.VMEM((2, chains, tq // V7X_MXU_WIDTH, tk, V7X_MXU_WIDTH), F32),
            pltpu.VMEM((2, chains, 1, tq), F32),
            pltpu.VMEM((chains, 1, tq), F32),
            pltpu.VMEM((chains, width + BF16_SUBLANES, tq), F32)]


def _attention_scratch_bytes(tq, chains, width):
    return chains * tq * tq * 4 + chains * (3 * 8 + width + BF16_SUBLANES) * tq * 4


def _attention_output(acc_scr, chains, width):
    return jnp.concatenate(
        [acc_scr[c, 0:width, :] / acc_scr[c, width:width + 1, :] for c in range(chains)], axis=0)


FOX_HEADS_PER_STEP = 4


def _fox_attn_kernel(qt_ref, qt_next_ref, k_ref, e_ref, vt_ref, o_ref,
                     qa_scr, s_scr, bmax_scr, m_scr, acc_scr):
    n_heads = FOX_HEADS_PER_STEP
    tk = s_scr.shape[3]
    first_head = n_heads * pl.program_id(1)
    qi = pl.program_id(2)
    cur = qi % 2

    def pair_lanes(h):
        return slice((h // 2) * V7X_LANES, (h // 2 + 1) * V7X_LANES)

    def put_features(slot, ref):
        row = lax.broadcasted_iota(jnp.int32, (V7X_LANES, ref.shape[1]), 0)
        for h in range(n_heads):
            q_t = ref[pair_lanes(h), :]
            own = (row >= (h % 2) * FOX_HEAD_DIM) & (row < (h % 2 + 1) * FOX_HEAD_DIM)
            qa_scr[slot, h, 0:V7X_LANES, :] = jnp.where(own, q_t, jnp.zeros_like(q_t))

    @pl.when(qi == 0)
    def _():
        put_features(0, qt_ref)
        row = lax.broadcasted_iota(jnp.int32, (V7X_LANES, qt_ref.shape[1]), 0)
        for h in range(n_heads):
            head = first_head + h
            bias_row = ((row == head) | (row == head + FOX_HEADS) | (row == head + 2 * FOX_HEADS))
            selector = jnp.where(bias_row, -1.0, 0.0).astype(BF16)
            qa_scr[0, h, V7X_LANES:, :] = selector
            qa_scr[1, h, V7X_LANES:, :] = selector

    put_features(1 - cur, qt_next_ref)

    def keys(j, h, nk):
        ks = pl.multiple_of(j * tk, tk)
        return jnp.concatenate([k_ref[0, pl.ds(ks, nk), pair_lanes(h)],
                                e_ref[0, pl.ds(ks, nk), :]], axis=1)

    def strip(n):
        return slice(n * V7X_MXU_WIDTH, (n + 1) * V7X_MXU_WIDTH)

    def score_fn(j, h, n, nk):
        return _dot(keys(j, h, nk), qa_scr[cur, h, :, strip(n)])

    def next_tile_score_fn(h, n):
        return _dot(keys(0, h, tk), qa_scr[1 - cur, h, :, strip(n)])

    def value_fn(j, h, nk):
        rows = slice(h * FOX_HEAD_DIM, (h + 1) * FOX_HEAD_DIM)
        return vt_ref[rows, pl.ds(pl.multiple_of(j * tk, tk), nk)]

    _causal_attention(qi, n_heads, score_fn, next_tile_score_fn,
                      value_fn, s_scr, bmax_scr, m_scr, acc_scr)
    o_ref[0] = _attention_output(acc_scr, n_heads, FOX_HEAD_DIM).T.astype(o_ref.dtype)


def _fox_attn(q_t, k, ext, v_t):
    b, s, _ = k.shape
    t = ATTN_Q_TILE
    n_tiles = s // t
    n_heads = FOX_HEADS_PER_STEP
    width = n_heads * FOX_HEAD_DIM
    pipelined = 3 * t * width * 2 + 2 * s * width * 2 + s * V7X_LANES * 2
    scratch = (2 * n_heads * 2 * V7X_LANES * t * 2
               + _attention_scratch_bytes(t, n_heads, FOX_HEAD_DIM))
    temps = scratch + n_heads * (t * t // 2 * (4 + 2) + t * V7X_LANES * 4)
    return pl.pallas_call(
        _fox_attn_kernel,
        grid=(b, FOX_HEADS // n_heads, n_tiles),
        in_specs=[
            pl.BlockSpec((width, t), lambda i, p, j: (p, i * n_tiles + j)),
            pl.BlockSpec((width, t),
                         lambda i, p, j: (p, i * n_tiles + jnp.minimum(j + 1, n_tiles - 1))),
            pl.BlockSpec((1, s, width), lambda i, p, j: (i, 0, p)),
            pl.BlockSpec((1, s, V7X_LANES), lambda i, p, j: (i, 0, 0)),
            pl.BlockSpec((width, s), lambda i, p, j: (p, i)),
        ],
        out_specs=pl.BlockSpec((1, t, width), lambda i, p, j: (i, j, p)),
        out_shape=jax.ShapeDtypeStruct((b, s, FOX_WIDTH), BF16),
        scratch_shapes=[pltpu.VMEM((2, n_heads, 2 * V7X_LANES, t), BF16)]
        + _attention_scratch(t, n_heads, FOX_HEAD_DIM),
        compiler_params=_compiler_params(3, pipelined, 0, temps),
        name="fox_attn",
    )(q_t, q_t, k, ext, v_t)


def _mix_ffn_kernel(x_ref, ctx_ref, wo_ref, g_ref, wu_ref, wd_ref, gf_ref, o_ref, *, final_norm):
    x1 = x_ref[...] + _dot(ctx_ref[...], wo_ref[...].astype(BF16))
    hn = (_rms_unit(x1) * g_ref[...]).astype(BF16)
    acc = x1
    for c in range(D_FF // FF_CHUNK):
        cols = slice(c * FF_CHUNK, (c + 1) * FF_CHUNK)
        up = jnp.maximum(_dot(hn, wu_ref[:, cols].astype(BF16)), 0.0)
        acc = acc + _dot((up * up).astype(BF16), wd_ref[cols, :].astype(BF16))
    if final_norm:
        acc = _rms_unit(acc) * gf_ref[...]
    o_ref[...] = acc


def _mix_ffn(x2d, ctx2d, w_out, g_ffn, w_up_all, w_down_all, layer, g_final, final_norm):
    m = x2d.shape[0]
    tm = ROW_TILE
    row = lambda i: (i, 0)
    fixed = lambda i: (0, 0)
    of_layer = lambda i: (layer, 0, 0)
    pipelined = tm * D_MODEL * (4 + 2 + 4)
    resident = (D_MODEL * D_MODEL + 2 * D_MODEL * D_FF) * 4 + 2 * D_MODEL * 4
    temps = (tm * D_MODEL * (4 + 4 + 2) + tm * FF_CHUNK * (4 + 4 + 2)
             + 2 * D_MODEL * FF_CHUNK * 2 + D_MODEL * D_MODEL * 2)
    return pl.pallas_call(
        functools.partial(_mix_ffn_kernel, final_norm=final_norm),
        grid=(m // tm,),
        in_specs=[
            pl.BlockSpec((tm, D_MODEL), row),
            pl.BlockSpec((tm, D_MODEL), row),
            _resident((D_MODEL, D_MODEL), fixed),
            _resident((1, D_MODEL), fixed),
            _resident((None, D_MODEL, D_FF), of_layer),
            _resident((None, D_FF, D_MODEL), of_layer),
            _resident((1, D_MODEL), fixed),
        ],
        out_specs=pl.BlockSpec((tm, D_MODEL), row),
        out_shape=jax.ShapeDtypeStruct((m, D_MODEL), F32),
        compiler_params=_compiler_params(1, pipelined, resident, temps),
        name="mix_ffn_final" if final_norm else "mix_ffn",
    )(x2d, ctx2d, w_out, g_ffn, w_up_all, w_down_all, g_final)


def _rope_lanes(t, cos, sin_signed):
    lane = lax.broadcasted_iota(jnp.int32, t.shape, 1)
    low = (lane % QK_ROPE_DIM) < (QK_ROPE_DIM // 2)
    partner = jnp.where(low,
                        pltpu.roll(t, V7X_LANES - QK_ROPE_DIM // 2, 1),
                        pltpu.roll(t, QK_ROPE_DIM // 2, 1))
    return t * cos + partner * sin_signed


def _rope_rows(t, cos_t, sin_signed_t):
    half = QK_ROPE_DIM // 2
    partner = jnp.concatenate([t[half:, :], t[:half, :]], axis=0)
    return t * cos_t + partner * sin_signed_t


def _mla_proj_kernel(x_ref, gkv_ref, gq_ref, wkva_ref, gkva_ref, wkb_ref, wvt_ref, wqa_ref,
                     gqa_ref, wqt_ref, cos_ref, sin_ref, cost_ref, sint_ref,
                     qcat_t_ref, kcat_ref, vt_ref, wa_scr):
    kva_cols = wkva_ref.shape[1]

    @pl.when(pl.program_id(0) == 0)
    def _():
        wa_scr[:, 0:kva_cols] = (wkva_ref[...] * gkv_ref[...]).astype(BF16)
        wa_scr[:, kva_cols:] = (wqa_ref[...] * gq_ref[...]).astype(BF16)

    unit = _rms_unit(x_ref[...])
    tm = unit.shape[0]
    nope_all = MLA_HEADS * QK_NOPE_DIM
    low_rank = _dot(unit.astype(BF16), wa_scr[...])

    kv_a = low_rank[:, 0:kva_cols]
    c_kv = (_rms_unit(kv_a[:, 0:KV_LORA_RANK]) * gkva_ref[...]).astype(BF16)
    k_rope = _rope_lanes(kv_a[:, KV_LORA_RANK:KV_LORA_RANK + V7X_LANES],
                         cos_ref[...], sin_ref[...])
    k_rope_even = k_rope.astype(BF16)
    k_rope_odd = pltpu.roll(k_rope, QK_ROPE_DIM, 1).astype(BF16)
    k_nope = _dot(c_kv, wkb_ref[...]).astype(BF16)
    vt_ref[...] = lax.dot_general(wvt_ref[...], c_kv, _NT, preferred_element_type=F32).astype(BF16)
    for head in range(MLA_HEADS):
        lo = head * MLA_QK_PAD
        kcat_ref[:, lo:lo + QK_NOPE_DIM] = k_nope[:, head * QK_NOPE_DIM:(head + 1) * QK_NOPE_DIM]
        kcat_ref[:, lo + QK_NOPE_DIM:lo + MLA_QK_PAD] = k_rope_odd if head % 2 else k_rope_even

    c_q = (_rms_unit(low_rank[:, kva_cols:]) * gqa_ref[...]).astype(BF16)
    q_nope_t = lax.dot_general(wqt_ref[0:nope_all, :], c_q, _NT, preferred_element_type=F32)
    q_rope_t = lax.dot_general(wqt_ref[nope_all:, :], c_q, _NT, preferred_element_type=F32)
    cos_t = cost_ref[...]
    sin_t = sint_ref[...]
    zero = jnp.zeros((QK_ROPE_DIM, tm), BF16)
    for head in range(MLA_HEADS):
        lo = head * MLA_QK_PAD
        nope = q_nope_t[head * QK_NOPE_DIM:(head + 1) * QK_NOPE_DIM, :]
        rope = _rope_rows(q_rope_t[head * QK_ROPE_DIM:(head + 1) * QK_ROPE_DIM, :], cos_t, sin_t)
        rope = (rope * MLA_QSCALE).astype(BF16)
        qcat_t_ref[lo:lo + QK_NOPE_DIM, :] = (nope * MLA_QSCALE).astype(BF16)
        qcat_t_ref[lo + QK_NOPE_DIM:lo + MLA_QK_PAD, :] = jnp.concatenate(
            [zero, rope] if head % 2 else [rope, zero], axis=0)


def _mla_proj(x2d, g_kv, g_q, w_kv_a, g_kv_a, w_k_b, w_vt_b, w_q_a, g_q_a, w_qt_b, rope, seq):
    m = x2d.shape[0]
    tm = ROW_TILE
    row = lambda i: (i, 0)
    col = lambda i: (0, i)
    fixed = lambda i: (0, 0)
    pos = lambda i: (i % (seq // tm), 0)
    pos_t = lambda i: (0, i % (seq // tm))
    kva_cols = w_kv_a.shape[1]
    cat = MLA_HEADS * MLA_QK_PAD
    pipelined = tm * D_MODEL * 4 + 2 * tm * V7X_LANES * 4 + tm * (2 * cat + D_MODEL) * 2
    resident = ((KV_LORA_RANK * 2 * D_MODEL + Q_LORA_RANK * 3 * D_MODEL // 2) * 2
                + D_MODEL * (kva_cols + Q_LORA_RANK) * 4 + 2 * D_MODEL * V7X_LANES * 4
                + 4 * D_MODEL * 4)
    scratch = D_MODEL * (kva_cols + Q_LORA_RANK) * 2
    temps = scratch + tm * D_MODEL * (4 + 2 + 2) + 4 * tm * D_MODEL * 4
    return pl.pallas_call(
        _mla_proj_kernel,
        grid=(m // tm,),
        in_specs=[
            pl.BlockSpec((tm, D_MODEL), row),
            _resident((D_MODEL, 1), fixed),
            _resident((D_MODEL, 1), fixed),
            _resident((D_MODEL, kva_cols), fixed),
            _resident((1, KV_LORA_RANK), fixed),
            _resident((KV_LORA_RANK, MLA_HEADS * QK_NOPE_DIM), fixed),
            _resident((MLA_HEADS * V_HEAD_DIM, KV_LORA_RANK), fixed),
            _resident((D_MODEL, Q_LORA_RANK), fixed),
            _resident((1, Q_LORA_RANK), fixed),
            _resident((3 * D_MODEL // 2, Q_LORA_RANK), fixed),
            pl.BlockSpec((tm, V7X_LANES), pos),
            pl.BlockSpec((tm, V7X_LANES), pos),
            pl.BlockSpec((QK_ROPE_DIM, tm), pos_t),
            pl.BlockSpec((QK_ROPE_DIM, tm), pos_t),
        ],
        out_specs=[
            pl.BlockSpec((cat, tm), col),
            pl.BlockSpec((tm, cat), row),
            pl.BlockSpec((MLA_HEADS * V_HEAD_DIM, tm), col),
        ],
        out_shape=[
            jax.ShapeDtypeStruct((cat, m), BF16),
            jax.ShapeDtypeStruct((m, cat), BF16),
            jax.ShapeDtypeStruct((MLA_HEADS * V_HEAD_DIM, m), BF16),
        ],
        scratch_shapes=[pltpu.VMEM((D_MODEL, kva_cols + Q_LORA_RANK), BF16)],
        compiler_params=_compiler_params(1, pipelined, resident, temps),
        name="mla_proj",
    )(x2d, g_kv, g_q, w_kv_a, g_kv_a, w_k_b, w_vt_b, w_q_a, g_q_a, w_qt_b, *rope)


MLA_HEADS_PER_STEP = 2


def _mla_attn_kernel(qt_ref, qt_next_ref, k_ref, vt_ref, o_ref, s_scr, bmax_scr, m_scr, acc_scr):
    n = MLA_HEADS_PER_STEP
    tk = s_scr.shape[3]

    def scores(q_ref, j, h, strip, nk):
        feats = slice(h * MLA_QK_PAD, (h + 1) * MLA_QK_PAD)
        k = k_ref[0, pl.ds(pl.multiple_of(j * tk, tk), nk), feats]
        return _dot(k, q_ref[feats, strip * V7X_MXU_WIDTH:(strip + 1) * V7X_MXU_WIDTH])

    def value_fn(j, h, nk):
        rows = slice(h * V_HEAD_DIM, (h + 1) * V_HEAD_DIM)
        return vt_ref[rows, pl.ds(pl.multiple_of(j * tk, tk), nk)]

    _causal_attention(pl.program_id(2), n,
                      functools.partial(scores, qt_ref),
                      lambda h, strip: scores(qt_next_ref, 0, h, strip, tk),
                      value_fn, s_scr, bmax_scr, m_scr, acc_scr)
    o_ref[0] = _attention_output(acc_scr, n, V_HEAD_DIM).T.astype(o_ref.dtype)


def _mla_attn(q_cat_t, k_cat, v_t, b, s):
    t = ATTN_Q_TILE
    n_tiles = s // t
    n = MLA_HEADS_PER_STEP
    pipelined = n * (2 * t * MLA_QK_PAD * 2 + s * MLA_QK_PAD * 2 + s * V_HEAD_DIM * 2
                     + t * V_HEAD_DIM * 2)
    temps = (_attention_scratch_bytes(t, n, V_HEAD_DIM)
             + n * (t * t // 2 * (4 + 2) + t * V_HEAD_DIM * 4))
    return pl.pallas_call(
        _mla_attn_kernel,
        grid=(b, MLA_HEADS // n, n_tiles),
        in_specs=[
            pl.BlockSpec((n * MLA_QK_PAD, t), lambda i, h, j: (h, i * n_tiles + j)),
            pl.BlockSpec((n * MLA_QK_PAD, t),
                         lambda i, h, j: (h, i * n_tiles + jnp.minimum(j + 1, n_tiles - 1))),
            pl.BlockSpec((1, s, n * MLA_QK_PAD), lambda i, h, j: (i, 0, h)),
            pl.BlockSpec((n * V_HEAD_DIM, s), lambda i, h, j: (h, i)),
        ],
        out_specs=pl.BlockSpec((1, t, n * V_HEAD_DIM), lambda i, h, j: (i, j, h)),
        out_shape=jax.ShapeDtypeStruct((b, s, MLA_HEADS * V_HEAD_DIM), BF16),
        scratch_shapes=_attention_scratch(t, n, V_HEAD_DIM),
        compiler_params=_compiler_params(3, pipelined, 0, temps),
        name="mla_attn",
    )(q_cat_t, q_cat_t, k_cat, v_t)


def _rope_tables(seq):
    f32 = np.float32
    inv = (f32(1.0) / (f32(ROPE_BASE) ** (np.arange(0, QK_ROPE_DIM, 2, dtype=f32)
                                          / f32(QK_ROPE_DIM)))).astype(f32)
    ang = np.arange(seq, dtype=f32)[:, None] * inv[None, :]
    cos, sin = np.cos(ang), np.sin(ang)
    cos_head = np.concatenate([cos, cos], axis=1)
    sin_head = np.concatenate([-sin, sin], axis=1)
    reps = V7X_LANES // QK_ROPE_DIM
    tables = (np.tile(cos_head, (1, reps)), np.tile(sin_head, (1, reps)), cos_head.T, sin_head.T)
    return tuple(jnp.asarray(np.ascontiguousarray(t), dtype=F32) for t in tables)


def _row(vec):
    return vec.reshape(1, -1).astype(F32)


def kernel(x, norm_mix_g, norm_ffn_g, fox_w_in, fox_b_f, fox_w_out, kv_norm_g, mla_w_kv_a,
           mla_kv_a_norm_g, mla_w_kv_b, mla_w_q_a, mla_q_a_norm_g, mla_w_q_b, mla_w_out,
           ffn_w_up, ffn_w_down, final_norm_g):
    b, s, d = x.shape
    assert d == D_MODEL and s % ATTN_Q_TILE == 0 and s % ROW_TILE == 0
    assert fox_w_in.shape[0] == 1 and mla_w_q_a.shape[0] == 1
    x2d = x.reshape(b * s, d)

    b_f = fox_b_f[0].astype(F32).reshape(FOX_HEADS, 1)
    q_t, k, v_t, ext = _fox_inproj(x2d, _row(norm_mix_g[0]), fox_w_in[0].T, b_f, s)
    ext = ext.reshape(b, s, V7X_LANES)
    ctx = _fox_attn(q_t, k.reshape(b, s, FOX_WIDTH), ext, v_t)
    x2d = _mix_ffn(x2d, ctx.reshape(b * s, FOX_WIDTH), fox_w_out[0],
                   _row(norm_ffn_g[0]), ffn_w_up, ffn_w_down, 0, _row(final_norm_g),
                   final_norm=False)

    w_kv_a = jnp.pad(mla_w_kv_a.astype(F32), ((0, 0), (0, V7X_LANES - QK_ROPE_DIM)))
    w_kv_b = mla_w_kv_b.reshape(KV_LORA_RANK, MLA_HEADS, QK_NOPE_DIM + V_HEAD_DIM)
    w_k_b = w_kv_b[:, :, :QK_NOPE_DIM].reshape(KV_LORA_RANK, -1).astype(BF16)
    w_vt_b = w_kv_b[:, :, QK_NOPE_DIM:].reshape(KV_LORA_RANK, -1).T.astype(BF16)
    w_q_b = mla_w_q_b[0].reshape(Q_LORA_RANK, MLA_HEADS, QK_NOPE_DIM + QK_ROPE_DIM)
    w_qt_b = jnp.concatenate(
        [w_q_b[:, :, :QK_NOPE_DIM].reshape(Q_LORA_RANK, -1),
         w_q_b[:, :, QK_NOPE_DIM:].reshape(Q_LORA_RANK, -1)], axis=1).T.astype(BF16)
    q_cat_t, k_cat, v_t = _mla_proj(
        x2d, kv_norm_g.astype(F32).reshape(D_MODEL, 1),
        norm_mix_g[1].astype(F32).reshape(D_MODEL, 1), w_kv_a, _row(mla_kv_a_norm_g), w_k_b,
        w_vt_b, mla_w_q_a[0].astype(F32), _row(mla_q_a_norm_g[0]), w_qt_b, _rope_tables(s), s)

    cat = MLA_HEADS * MLA_QK_PAD
    ctx = _mla_attn(q_cat_t, k_cat.reshape(b, s, cat), v_t, b, s)
    out = _mix_ffn(x2d, ctx.reshape(b * s, MLA_HEADS * V_HEAD_DIM), mla_w_out[0],
                   _row(norm_ffn_g[1]), ffn_w_up, ffn_w_down, 1, _row(final_norm_g),
                   final_norm=True)
    return out.reshape(b, s, d)
```

```python
import functools

import jax
import jax.numpy as jnp
import numpy as np
from jax import lax
from jax.experimental import pallas as pl
from jax.experimental.pallas import tpu as pltpu

D_MODEL = 1024
FOX_HEADS = 16
FOX_HEAD_DIM = 64
FOX_WIDTH = FOX_HEADS * FOX_HEAD_DIM
MLA_HEADS = 8
QK_NOPE_DIM = 128
QK_ROPE_DIM = 64
V_HEAD_DIM = 128
Q_LORA_RANK = 384
KV_LORA_RANK = 256
ROPE_BASE = 10000.0
D_FF = 4 * D_MODEL
EPS = 1e-6

V7X_LANES = 128
V7X_VMEM_BYTES = 64 * 1024 * 1024
BF16_SUBLANES = 16
V7X_MXU_WIDTH = 256

LOG2E = 1.4426950408889634
FOX_QSCALE = (FOX_HEAD_DIM ** -0.5) * LOG2E
MLA_QSCALE = ((QK_NOPE_DIM + QK_ROPE_DIM) ** -0.5) * LOG2E
MLA_QK_PAD = 256

ROW_TILE = 512
ATTN_Q_TILE = 1024
FF_CHUNK = 512

BF16 = jnp.bfloat16
F32 = jnp.float32
_NT = (((1,), (1,)), ((), ()))


def _compiler_params(n_grid, pipelined_bytes, resident_bytes, temp_bytes):
    need = 2 * pipelined_bytes + resident_bytes + temp_bytes
    return pltpu.CompilerParams(
        dimension_semantics=("arbitrary",) * n_grid,
        vmem_limit_bytes=min(need, V7X_VMEM_BYTES),
    )


def _resident(shape, index_map):
    return pl.BlockSpec(shape, index_map, pipeline_mode=pl.Buffered(1))


def _rms_unit(xf):
    return xf * lax.rsqrt(jnp.mean(xf * xf, axis=-1, keepdims=True) + EPS)


def _dot(a, b):
    return jnp.dot(a, b, preferred_element_type=F32)


def _split3(x):
    hi = x.astype(BF16)
    rem = x - hi.astype(F32)
    mid = rem.astype(BF16)
    lo = (rem - mid.astype(F32)).astype(BF16)
    return hi, mid, lo


def _log_sigmoid_pieces(z_t):
    ls = jnp.minimum(z_t, 0.0) - jnp.log1p(jnp.exp(-jnp.abs(z_t)))
    return jnp.concatenate(_split3(ls), axis=0)


def _decay_extension(pieces_t, carry_ref):
    heads = FOX_HEADS
    tokens = pieces_t.shape[1]
    row = lax.broadcasted_iota(jnp.int32, (tokens, tokens), 0)
    col = lax.broadcasted_iota(jnp.int32, (tokens, tokens), 1)
    upper = jnp.where(row <= col, 1.0, 0.0).astype(BF16)
    sums = _dot(pieces_t, upper)
    c_t = sums[0:heads] + sums[heads:2 * heads] + sums[2 * heads:3 * heads] + carry_ref[...]
    last = lax.broadcasted_iota(jnp.int32, c_t.shape, 1) == tokens - 1
    carry_ref[...] = jnp.sum(jnp.where(last, c_t, 0.0), axis=1, keepdims=True)
    stacked = jnp.concatenate(
        [piece.astype(F32) for piece in _split3(c_t * LOG2E)]
        + [jnp.zeros((V7X_LANES - 3 * heads, tokens), F32)], axis=0)
    return stacked.T.astype(BF16)


def _fox_inproj_kernel(x_ref, g_ref, w_ref, b_ref, qt_ref, k_ref, vt_ref, e_ref,
                       wk_scr, wgqv_scr, carry_scr, *, tiles_per_seq):
    heads = FOX_HEADS
    @pl.when(pl.program_id(0) % tiles_per_seq == 0)
    def _():
        carry_scr[...] = jnp.zeros_like(carry_scr)

    @pl.when(pl.program_id(0) == 0)
    def _():
        chunk = V7X_MXU_WIDTH
        wgqv_scr[0:heads, :] = w_ref[3 * FOX_WIDTH:3 * FOX_WIDTH + heads, :].astype(BF16)
        wgqv_scr[heads:heads + FOX_WIDTH, :] = w_ref[0:FOX_WIDTH, :].astype(BF16)
        wgqv_scr[heads + FOX_WIDTH:, :] = w_ref[2 * FOX_WIDTH:3 * FOX_WIDTH, :].astype(BF16)
        for c in range(0, FOX_WIDTH, chunk):
            block = w_ref[FOX_WIDTH + c:FOX_WIDTH + c + chunk, :]
            wk_scr[:, c:c + chunk] = block.T.astype(BF16)

    h = (_rms_unit(x_ref[...]) * g_ref[...]).astype(BF16)
    gq_t = lax.dot_general(wgqv_scr[0:heads + FOX_WIDTH, :], h, _NT, preferred_element_type=F32)
    gate_pieces = _log_sigmoid_pieces(gq_t[0:heads, :] + b_ref[...])
    qt_ref[...] = (gq_t[heads:, :] * FOX_QSCALE).astype(BF16)
    k_ref[...] = _dot(h, wk_scr[...]).astype(BF16)
    e_ref[...] = _decay_extension(gate_pieces, carry_scr)
    v_t = lax.dot_general(wgqv_scr[heads + FOX_WIDTH:, :], h, _NT, preferred_element_type=F32)
    vt_ref[...] = v_t.astype(BF16)


def _fox_inproj(x2d, g, w_in_t, b_f, seq):
    m = x2d.shape[0]
    tm = ROW_TILE
    in_cols = w_in_t.shape[0]
    row = lambda i: (i, 0)
    col = lambda i: (0, i)
    fixed = lambda i: (0, 0)
    feature_major = jax.ShapeDtypeStruct((FOX_WIDTH, m), BF16)
    gqv_rows = FOX_HEADS + 2 * FOX_WIDTH
    pipelined = tm * D_MODEL * 4 + 3 * tm * FOX_WIDTH * 2 + tm * V7X_LANES * 2
    resident = D_MODEL * 4 + D_MODEL * in_cols * 4 + FOX_HEADS * V7X_LANES * 4
    scratch = D_MODEL * (FOX_WIDTH + gqv_rows) * 2 + FOX_HEADS * V7X_LANES * 4
    temps = (scratch + tm * D_MODEL * (4 + 2) + 2 * tm * FOX_WIDTH * 4
             + 2 * tm * tm * 4 + 4 * tm * V7X_LANES * 4)
    return pl.pallas_call(
        functools.partial(_fox_inproj_kernel, tiles_per_seq=seq // tm),
        grid=(m // tm,),
        in_specs=[
            pl.BlockSpec((tm, D_MODEL), row),
            _resident((1, D_MODEL), fixed),
            _resident((in_cols, D_MODEL), fixed),
            _resident((FOX_HEADS, 1), fixed),
        ],
        out_specs=[
            pl.BlockSpec((FOX_WIDTH, tm), col),
            pl.BlockSpec((tm, FOX_WIDTH), row),
            pl.BlockSpec((FOX_WIDTH, tm), col),
            pl.BlockSpec((tm, V7X_LANES), row),
        ],
        out_shape=[feature_major, jax.ShapeDtypeStruct((m, FOX_WIDTH), BF16), feature_major,
                   jax.ShapeDtypeStruct((m, V7X_LANES), BF16)],
        scratch_shapes=[pltpu.VMEM((D_MODEL, FOX_WIDTH), BF16),
                        pltpu.VMEM((gqv_rows, D_MODEL), BF16),
                        pltpu.VMEM((FOX_HEADS, 1), F32)],
        compiler_params=_compiler_params(1, pipelined, resident, temps),
        name="fox_inproj",
    )(x2d, g, w_in_t, b_f)


def _causal_attention(qi, chains, score_fn, next_tile_score_fn, value_fn,
                      s_scr, bmax_scr, m_scr, acc_scr):
    _, _, strips, tk, strip_width = s_scr.shape
    assert strip_width == V7X_MXU_WIDTH and strips * strip_width == 2 * tk
    m_scr[...] = jnp.full(m_scr.shape, -jnp.inf, F32)
    acc_scr[...] = jnp.zeros(acc_scr.shape, F32)
    ones = jnp.ones((BF16_SUBLANES, tk), BF16)

    def lanes(n):
        return slice(n * V7X_MXU_WIDTH, (n + 1) * V7X_MXU_WIDTH)

    def store_scores(s_t, slot, c, n):
        s_scr[slot, c, n] = s_t
        bmax_scr[slot, c, :, lanes(n)] = jnp.max(s_t, axis=0, keepdims=True)

    def issue(j, slot, c, n):
        store_scores(score_fn(j, c, n, tk), slot, c, n)

    def visible_keys(n, key_offset):
        return min(tk, (n + 1) * V7X_MXU_WIDTH - key_offset)

    def issue_masked(j, slot, c, n, key_offset):
        nk = visible_keys(n, key_offset)
        s_scr[slot, c, n, 0:nk, :] = score_fn(j, c, n, nk)

    def consume(j, slot, c, n, key_offset=None):
        nk = tk if key_offset is None else visible_keys(n, key_offset)
        s_t = s_scr[slot, c, n, 0:nk, :]
        if key_offset is None:
            block_max = bmax_scr[slot, c, :, lanes(n)]
        else:
            key = lax.broadcasted_iota(jnp.int32, s_t.shape, 0) + key_offset
            query = lax.broadcasted_iota(jnp.int32, s_t.shape, 1) + n * V7X_MXU_WIDTH
            s_t = jnp.where(key <= query, s_t, -jnp.inf)
            block_max = jnp.max(s_t, axis=0, keepdims=True)
        m = m_scr[c, :, lanes(n)]
        m_new = jnp.maximum(m, block_max)
        p_t = jnp.exp2(s_t - m_new).astype(BF16)
        m_scr[c, :, lanes(n)] = m_new
        v_ones = jnp.concatenate([value_fn(j, c, nk), ones[:, 0:nk]], axis=0)
        acc_scr[c, :, lanes(n)] = (jnp.exp2(m - m_new) * acc_scr[c, :, lanes(n)]
                                   + _dot(v_ones, p_t))

    units = [(c, n) for n in range(strips) for c in range(chains)]

    def consume_and_issue(j, slot):
        for c, n in units:
            consume(j, slot, c, n)
            issue(j + 1, 1 - slot, c, n)

    @pl.when(qi == 0)
    def _():
        for c, n in units:
            issue(0, 0, c, n)

    def two_blocks(i):
        consume_and_issue(2 * i, 0)
        consume_and_issue(2 * i + 1, 1)

    def four_blocks(i, carry):
        two_blocks(2 * i)
        two_blocks(2 * i + 1)
        return carry

    lax.fori_loop(0, qi // 2, four_blocks, 0)

    @pl.when(qi % 2 == 1)
    def _():
        two_blocks(qi - 1)

    half = strips // 2
    for c, n in units:
        consume(2 * qi, 0, c, n, key_offset=0 if n < half else None)
        if n >= half:
            issue_masked(2 * qi + 1, 1, c, n, tk)
        else:
            store_scores(next_tile_score_fn(c, n), 0, c, n)
    for c, n in units:
        if n >= half:
            consume(2 * qi + 1, 1, c, n, key_offset=tk)
            store_scores(next_tile_score_fn(c, n), 0, c, n)


def _attention_scratch(tq, chains, width):
    tk = tq // 2
    return [pltpu.VMEM((2, chains, tq // V7X_MXU_WIDTH, tk, V7X_MXU_WIDTH), F32),
            pltpu.VMEM((2, chains, 1, tq), F32),
            pltpu.VMEM((chains, 1, tq), F32),
            pltpu.VMEM((chains, width + BF16_SUBLANES, tq), F32)]


def _attention_scratch_bytes(tq, chains, width):
    return chains * tq * tq * 4 + chains * (3 * 8 + width + BF16_SUBLANES) * tq * 4


def _attention_output(acc_scr, chains, width):
    return jnp.concatenate(
        [acc_scr[c, 0:width, :] / acc_scr[c, width:width + 1, :] for c in range(chains)], axis=0)


FOX_HEADS_PER_STEP = 4


def _fox_attn_kernel(qt_ref, qt_next_ref, k_ref, e_ref, vt_ref, o_ref,
                     qa_scr, s_scr, bmax_scr, m_scr, acc_scr):
    n_heads = FOX_HEADS_PER_STEP
    tk = s_scr.shape[3]
    first_head = n_heads * pl.program_id(1)
    qi = pl.program_id(2)
    cur = qi % 2

    def pair_lanes(h):
        return slice((h // 2) * V7X_LANES, (h // 2 + 1) * V7X_LANES)

    def put_features(slot, ref):
        row = lax.broadcasted_iota(jnp.int32, (V7X_LANES, ref.shape[1]), 0)
        for h in range(n_heads):
            q_t = ref[pair_lanes(h), :]
            own = (row >= (h % 2) * FOX_HEAD_DIM) & (row < (h % 2 + 1) * FOX_HEAD_DIM)
            qa_scr[slot, h, 0:V7X_LANES, :] = jnp.where(own, q_t, jnp.zeros_like(q_t))

    @pl.when(qi == 0)
    def _():
        put_features(0, qt_ref)
        row = lax.broadcasted_iota(jnp.int32, (V7X_LANES, qt_ref.shape[1]), 0)
        for h in range(n_heads):
            head = first_head + h
            bias_row = ((row == head) | (row == head + FOX_HEADS) | (row == head + 2 * FOX_HEADS))
            selector = jnp.where(bias_row, -1.0, 0.0).astype(BF16)
            qa_scr[0, h, V7X_LANES:, :] = selector
            qa_scr[1, h, V7X_LANES:, :] = selector

    put_features(1 - cur, qt_next_ref)

    def keys(j, h, nk):
        ks = pl.multiple_of(j * tk, tk)
        return jnp.concatenate([k_ref[0, pl.ds(ks, nk), pair_lanes(h)],
                                e_ref[0, pl.ds(ks, nk), :]], axis=1)

    def strip(n):
        return slice(n * V7X_MXU_WIDTH, (n + 1) * V7X_MXU_WIDTH)

    def score_fn(j, h, n, nk):
        return _dot(keys(j, h, nk), qa_scr[cur, h, :, strip(n)])

    def next_tile_score_fn(h, n):
        return _dot(keys(0, h, tk), qa_scr[1 - cur, h, :, strip(n)])

    def value_fn(j, h, nk):
        rows = slice(h * FOX_HEAD_DIM, (h + 1) * FOX_HEAD_DIM)
        return vt_ref[rows, pl.ds(pl.multiple_of(j * tk, tk), nk)]

    _causal_attention(qi, n_heads, score_fn, next_tile_score_fn,
                      value_fn, s_scr, bmax_scr, m_scr, acc_scr)
    o_ref[...] = _attention_output(acc_scr, n_heads, FOX_HEAD_DIM).astype(o_ref.dtype)


def _fox_attn(q_t, k, ext, v_t):
    b, s, _ = k.shape
    t = ATTN_Q_TILE
    n_tiles = s // t
    n_heads = FOX_HEADS_PER_STEP
    width = n_heads * FOX_HEAD_DIM
    pipelined = 3 * t * width * 2 + 2 * s * width * 2 + s * V7X_LANES * 2
    scratch = (2 * n_heads * 2 * V7X_LANES * t * 2
               + _attention_scratch_bytes(t, n_heads, FOX_HEAD_DIM))
    temps = scratch + n_heads * (t * t // 2 * (4 + 2) + t * V7X_LANES * 4)
    return pl.pallas_call(
        _fox_attn_kernel,
        grid=(b, FOX_HEADS // n_heads, n_tiles),
        in_specs=[
            pl.BlockSpec((width, t), lambda i, p, j: (p, i * n_tiles + j)),
            pl.BlockSpec((width, t),
                         lambda i, p, j: (p, i * n_tiles + jnp.minimum(j + 1, n_tiles - 1))),
            pl.BlockSpec((1, s, width), lambda i, p, j: (i, 0, p)),
            pl.BlockSpec((1, s, V7X_LANES), lambda i, p, j: (i, 0, 0)),
            pl.BlockSpec((width, s), lambda i, p, j: (p, i)),
        ],
        out_specs=pl.BlockSpec((width, t), lambda i, p, j: (p, i * n_tiles + j)),
        out_shape=jax.ShapeDtypeStruct((FOX_WIDTH, b * s), BF16),
        scratch_shapes=[pltpu.VMEM((2, n_heads, 2 * V7X_LANES, t), BF16)]
        + _attention_scratch(t, n_heads, FOX_HEAD_DIM),
        compiler_params=_compiler_params(3, pipelined, 0, temps),
        name="fox_attn",
    )(q_t, q_t, k, ext, v_t)


def _mix_ffn_kernel(x_ref, ctx_ref, wo_ref, g_ref, wu_ref, wd_ref, gf_ref, o_ref, *, final_norm):
    x1 = x_ref[...] + lax.dot_general(ctx_ref[...], wo_ref[...].astype(BF16),
                                      (((0,), (0,)), ((), ())), preferred_element_type=F32)
    hn = (_rms_unit(x1) * g_ref[...]).astype(BF16)
    acc = x1
    for c in range(D_FF // FF_CHUNK):
        cols = slice(c * FF_CHUNK, (c + 1) * FF_CHUNK)
        up = jnp.maximum(_dot(hn, wu_ref[:, cols].astype(BF16)), 0.0)
        acc = acc + _dot((up * up).astype(BF16), wd_ref[cols, :].astype(BF16))
    if final_norm:
        acc = _rms_unit(acc) * gf_ref[...]
    o_ref[...] = acc


def _mix_ffn(x2d, ctx2d, w_out, g_ffn, w_up_all, w_down_all, layer, g_final, final_norm):
    m = x2d.shape[0]
    tm = ROW_TILE
    row = lambda i: (i, 0)
    fixed = lambda i: (0, 0)
    of_layer = lambda i: (layer, 0, 0)
    pipelined = tm * D_MODEL * (4 + 2 + 4)
    resident = (D_MODEL * D_MODEL + 2 * D_MODEL * D_FF) * 4 + 2 * D_MODEL * 4
    temps = (tm * D_MODEL * (4 + 4 + 2) + tm * FF_CHUNK * (4 + 4 + 2)
             + 2 * D_MODEL * FF_CHUNK * 2 + D_MODEL * D_MODEL * 2)
    return pl.pallas_call(
        functools.partial(_mix_ffn_kernel, final_norm=final_norm),
        grid=(m // tm,),
        in_specs=[
            pl.BlockSpec((tm, D_MODEL), row),
            pl.BlockSpec((D_MODEL, tm), lambda i: (0, i)),
            _resident((D_MODEL, D_MODEL), fixed),
            _resident((1, D_MODEL), fixed),
            _resident((None, D_MODEL, D_FF), of_layer),
            _resident((None, D_FF, D_MODEL), of_layer),
            _resident((1, D_MODEL), fixed),
        ],
        out_specs=pl.BlockSpec((tm, D_MODEL), row),
        out_shape=jax.ShapeDtypeStruct((m, D_MODEL), F32),
        compiler_params=_compiler_params(1, pipelined, resident, temps),
        name="mix_ffn_final" if final_norm else "mix_ffn",
    )(x2d, ctx2d, w_out, g_ffn, w_up_all, w_down_all, g_final)


def _rope_lanes(t, cos, sin_signed):
    lane = lax.broadcasted_iota(jnp.int32, t.shape, 1)
    low = (lane % QK_ROPE_DIM) < (QK_ROPE_DIM // 2)
    partner = jnp.where(low,
                        pltpu.roll(t, V7X_LANES - QK_ROPE_DIM // 2, 1),
                        pltpu.roll(t, QK_ROPE_DIM // 2, 1))
    return t * cos + partner * sin_signed


def _rope_rows(t, cos_t, sin_signed_t):
    half = QK_ROPE_DIM // 2
    partner = jnp.concatenate([t[half:, :], t[:half, :]], axis=0)
    return t * cos_t + partner * sin_signed_t


def _mla_proj_kernel(x_ref, gkv_ref, gq_ref, wkva_ref, gkva_ref, wkb_ref, wvt_ref, wqa_ref,
                     gqa_ref, wqt_ref, cos_ref, sin_ref, cost_ref, sint_ref,
                     qcat_t_ref, kcat_ref, vt_ref, wa_scr):
    kva_cols = wkva_ref.shape[1]

    @pl.when(pl.program_id(0) == 0)
    def _():
        wa_scr[:, 0:kva_cols] = (wkva_ref[...] * gkv_ref[...]).astype(BF16)
        wa_scr[:, kva_cols:] = (wqa_ref[...] * gq_ref[...]).astype(BF16)

    unit = _rms_unit(x_ref[...])
    tm = unit.shape[0]
    nope_all = MLA_HEADS * QK_NOPE_DIM
    low_rank = _dot(unit.astype(BF16), wa_scr[...])

    kv_a = low_rank[:, 0:kva_cols]
    c_kv = (_rms_unit(kv_a[:, 0:KV_LORA_RANK]) * gkva_ref[...]).astype(BF16)
    k_rope = _rope_lanes(kv_a[:, KV_LORA_RANK:KV_LORA_RANK + V7X_LANES],
                         cos_ref[...], sin_ref[...])
    k_rope_even = k_rope.astype(BF16)
    k_rope_odd = pltpu.roll(k_rope, QK_ROPE_DIM, 1).astype(BF16)
    k_nope = _dot(c_kv, wkb_ref[...]).astype(BF16)
    vt_ref[...] = lax.dot_general(wvt_ref[...], c_kv, _NT, preferred_element_type=F32).astype(BF16)
    for head in range(MLA_HEADS):
        lo = head * MLA_QK_PAD
        kcat_ref[:, lo:lo + QK_NOPE_DIM] = k_nope[:, head * QK_NOPE_DIM:(head + 1) * QK_NOPE_DIM]
        kcat_ref[:, lo + QK_NOPE_DIM:lo + MLA_QK_PAD] = k_rope_odd if head % 2 else k_rope_even

    c_q = (_rms_unit(low_rank[:, kva_cols:]) * gqa_ref[...]).astype(BF16)
    q_nope_t = lax.dot_general(wqt_ref[0:nope_all, :], c_q, _NT, preferred_element_type=F32)
    q_rope_t = lax.dot_general(wqt_ref[nope_all:, :], c_q, _NT, preferred_element_type=F32)
    cos_t = cost_ref[...]
    sin_t = sint_ref[...]
    zero = jnp.zeros((QK_ROPE_DIM, tm), BF16)
    for head in range(MLA_HEADS):
        lo = head * MLA_QK_PAD
        nope = q_nope_t[head * QK_NOPE_DIM:(head + 1) * QK_NOPE_DIM, :]
        rope = _rope_rows(q_rope_t[head * QK_ROPE_DIM:(head + 1) * QK_ROPE_DIM, :], cos_t, sin_t)
        rope = (rope * MLA_QSCALE).astype(BF16)
        qcat_t_ref[lo:lo + QK_NOPE_DIM, :] = (nope * MLA_QSCALE).astype(BF16)
        qcat_t_ref[lo + QK_NOPE_DIM:lo + MLA_QK_PAD, :] = jnp.concatenate(
            [zero, rope] if head % 2 else [rope, zero], axis=0)


def _mla_proj(x2d, g_kv, g_q, w_kv_a, g_kv_a, w_k_b, w_vt_b, w_q_a, g_q_a, w_qt_b, rope, seq):
    m = x2d.shape[0]
    tm = ROW_TILE
    row = lambda i: (i, 0)
    col = lambda i: (0, i)
    fixed = lambda i: (0, 0)
    pos = lambda i: (i % (seq // tm), 0)
    pos_t = lambda i: (0, i % (seq // tm))
    kva_cols = w_kv_a.shape[1]
    cat = MLA_HEADS * MLA_QK_PAD
    pipelined = tm * D_MODEL * 4 + 2 * tm * V7X_LANES * 4 + tm * (2 * cat + D_MODEL) * 2
    resident = ((KV_LORA_RANK * 2 * D_MODEL + Q_LORA_RANK * 3 * D_MODEL // 2) * 2
                + D_MODEL * (kva_cols + Q_LORA_RANK) * 4 + 2 * D_MODEL * V7X_LANES * 4
                + 4 * D_MODEL * 4)
    scratch = D_MODEL * (kva_cols + Q_LORA_RANK) * 2
    temps = scratch + tm * D_MODEL * (4 + 2 + 2) + 4 * tm * D_MODEL * 4
    return pl.pallas_call(
        _mla_proj_kernel,
        grid=(m // tm,),
        in_specs=[
            pl.BlockSpec((tm, D_MODEL), row),
            _resident((D_MODEL, 1), fixed),
            _resident((D_MODEL, 1), fixed),
            _resident((D_MODEL, kva_cols), fixed),
            _resident((1, KV_LORA_RANK), fixed),
            _resident((KV_LORA_RANK, MLA_HEADS * QK_NOPE_DIM), fixed),
            _resident((MLA_HEADS * V_HEAD_DIM, KV_LORA_RANK), fixed),
            _resident((D_MODEL, Q_LORA_RANK), fixed),
            _resident((1, Q_LORA_RANK), fixed),
            _resident((3 * D_MODEL // 2, Q_LORA_RANK), fixed),
            pl.BlockSpec((tm, V7X_LANES), pos),
            pl.BlockSpec((tm, V7X_LANES), pos),
            pl.BlockSpec((QK_ROPE_DIM, tm), pos_t),
            pl.BlockSpec((QK_ROPE_DIM, tm), pos_t),
        ],
        out_specs=[
            pl.BlockSpec((cat, tm), col),
            pl.BlockSpec((tm, cat), row),
            pl.BlockSpec((MLA_HEADS * V_HEAD_DIM, tm), col),
        ],
        out_shape=[
            jax.ShapeDtypeStruct((cat, m), BF16),
            jax.ShapeDtypeStruct((m, cat), BF16),
            jax.ShapeDtypeStruct((MLA_HEADS * V_HEAD_DIM, m), BF16),
        ],
        scratch_shapes=[pltpu.VMEM((D_MODEL, kva_cols + Q_LORA_RANK), BF16)],
        compiler_params=_compiler_params(1, pipelined, resident, temps),
        name="mla_proj",
    )(x2d, g_kv, g_q, w_kv_a, g_kv_a, w_k_b, w_vt_b, w_q_a, g_q_a, w_qt_b, *rope)


MLA_HEADS_PER_STEP = 2


def _mla_attn_kernel(qt_ref, qt_next_ref, k_ref, vt_ref, o_ref, s_scr, bmax_scr, m_scr, acc_scr):
    n = MLA_HEADS_PER_STEP
    tk = s_scr.shape[3]

    def scores(q_ref, j, h, strip, nk):
        feats = slice(h * MLA_QK_PAD, (h + 1) * MLA_QK_PAD)
        k = k_ref[0, pl.ds(pl.multiple_of(j * tk, tk), nk), feats]
        return _dot(k, q_ref[feats, strip * V7X_MXU_WIDTH:(strip + 1) * V7X_MXU_WIDTH])

    def value_fn(j, h, nk):
        rows = slice(h * V_HEAD_DIM, (h + 1) * V_HEAD_DIM)
        return vt_ref[rows, pl.ds(pl.multiple_of(j * tk, tk), nk)]

    _causal_attention(pl.program_id(2), n,
                      functools.partial(scores, qt_ref),
                      lambda h, strip: scores(qt_next_ref, 0, h, strip, tk),
                      value_fn, s_scr, bmax_scr, m_scr, acc_scr)
    o_ref[...] = _attention_output(acc_scr, n, V_HEAD_DIM).astype(o_ref.dtype)


def _mla_attn(q_cat_t, k_cat, v_t, b, s):
    t = ATTN_Q_TILE
    n_tiles = s // t
    n = MLA_HEADS_PER_STEP
    pipelined = n * (2 * t * MLA_QK_PAD * 2 + s * MLA_QK_PAD * 2 + s * V_HEAD_DIM * 2
                     + t * V_HEAD_DIM * 2)
    temps = (_attention_scratch_bytes(t, n, V_HEAD_DIM)
             + n * (t * t // 2 * (4 + 2) + t * V_HEAD_DIM * 4))
    return pl.pallas_call(
        _mla_attn_kernel,
        grid=(b, MLA_HEADS // n, n_tiles),
        in_specs=[
            pl.BlockSpec((n * MLA_QK_PAD, t), lambda i, h, j: (h, i * n_tiles + j)),
            pl.BlockSpec((n * MLA_QK_PAD, t),
                         lambda i, h, j: (h, i * n_tiles + jnp.minimum(j + 1, n_tiles - 1))),
            pl.BlockSpec((1, s, n * MLA_QK_PAD), lambda i, h, j: (i, 0, h)),
            pl.BlockSpec((n * V_HEAD_DIM, s), lambda i, h, j: (h, i)),
        ],
        out_specs=pl.BlockSpec((n * V_HEAD_DIM, t), lambda i, h, j: (h, i * n_tiles + j)),
        out_shape=jax.ShapeDtypeStruct((MLA_HEADS * V_HEAD_DIM, b * s), BF16),
        scratch_shapes=_attention_scratch(t, n, V_HEAD_DIM),
        compiler_params=_compiler_params(3, pipelined, 0, temps),
        name="mla_attn",
    )(q_cat_t, q_cat_t, k_cat, v_t)


def _rope_tables(seq):
    f32 = np.float32
    inv = (f32(1.0) / (f32(ROPE_BASE) ** (np.arange(0, QK_ROPE_DIM, 2, dtype=f32)
                                          / f32(QK_ROPE_DIM)))).astype(f32)
    ang = np.arange(seq, dtype=f32)[:, None] * inv[None, :]
    cos, sin = np.cos(ang), np.sin(ang)
    cos_head = np.concatenate([cos, cos], axis=1)
    sin_head = np.concatenate([-sin, sin], axis=1)
    reps = V7X_LANES // QK_ROPE_DIM
    tables = (np.tile(cos_head, (1, reps)), np.tile(sin_head, (1, reps)), cos_head.T, sin_head.T)
    return tuple(jnp.asarray(np.ascontiguousarray(t), dtype=F32) for t in tables)


def _row(vec):
    return vec.reshape(1, -1).astype(F32)


def kernel(x, norm_mix_g, norm_ffn_g, fox_w_in, fox_b_f, fox_w_out, kv_norm_g, mla_w_kv_a,
           mla_kv_a_norm_g, mla_w_kv_b, mla_w_q_a, mla_q_a_norm_g, mla_w_q_b, mla_w_out,
           ffn_w_up, ffn_w_down, final_norm_g):
    b, s, d = x.shape
    assert d == D_MODEL and s % ATTN_Q_TILE == 0 and s % ROW_TILE == 0
    assert fox_w_in.shape[0] == 1 and mla_w_q_a.shape[0] == 1
    x2d = x.reshape(b * s, d)

    b_f = fox_b_f[0].astype(F32).reshape(FOX_HEADS, 1)
    q_t, k, v_t, ext = _fox_inproj(x2d, _row(norm_mix_g[0]), fox_w_in[0].T, b_f, s)
    ext = ext.reshape(b, s, V7X_LANES)
    ctx = _fox_attn(q_t, k.reshape(b, s, FOX_WIDTH), ext, v_t)
    x2d = _mix_ffn(x2d, ctx, fox_w_out[0],
                   _row(norm_ffn_g[0]), ffn_w_up, ffn_w_down, 0, _row(final_norm_g),
                   final_norm=False)

    w_kv_a = jnp.pad(mla_w_kv_a.astype(F32), ((0, 0), (0, V7X_LANES - QK_ROPE_DIM)))
    w_kv_b = mla_w_kv_b.reshape(KV_LORA_RANK, MLA_HEADS, QK_NOPE_DIM + V_HEAD_DIM)
    w_k_b = w_kv_b[:, :, :QK_NOPE_DIM].reshape(KV_LORA_RANK, -1).astype(BF16)
    w_vt_b = w_kv_b[:, :, QK_NOPE_DIM:].reshape(KV_LORA_RANK, -1).T.astype(BF16)
    w_q_b = mla_w_q_b[0].reshape(Q_LORA_RANK, MLA_HEADS, QK_NOPE_DIM + QK_ROPE_DIM)
    w_qt_b = jnp.concatenate(
        [w_q_b[:, :, :QK_NOPE_DIM].reshape(Q_LORA_RANK, -1),
         w_q_b[:, :, QK_NOPE_DIM:].reshape(Q_LORA_RANK, -1)], axis=1).T.astype(BF16)
    q_cat_t, k_cat, v_t = _mla_proj(
        x2d, kv_norm_g.astype(F32).reshape(D_MODEL, 1),
        norm_mix_g[1].astype(F32).reshape(D_MODEL, 1), w_kv_a, _row(mla_kv_a_norm_g), w_k_b,
        w_vt_b, mla_w_q_a[0].astype(F32), _row(mla_q_a_norm_g[0]), w_qt_b, _rope_tables(s), s)

    cat = MLA_HEADS * MLA_QK_PAD
    ctx = _mla_attn(q_cat_t, k_cat.reshape(b, s, cat), v_t, b, s)
    out = _mix_ffn(x2d, ctx, mla_w_out[0],
                   _row(norm_ffn_g[1]), ffn_w_up, ffn_w_down, 1, _row(final_norm_g),
                   final_norm=True)
    return out.reshape(b, s, d)
```
